```python
import math
import jax, jax.numpy as jnp
from jax import lax
import numpy as np

D_MODEL = 2048
BATCH = 4
SEQ = 2048
DEPTH = 1

CHUNK = 64
MIX_WIDTH = D_MODEL
RET_WIDTH = MIX_WIDTH // 2
ATT_WIDTH = MIX_WIDTH - RET_WIDTH
RET_HEAD_DIM = 256
RET_HEADS = RET_WIDTH // RET_HEAD_DIM
ATT_HEAD_DIM = 128
ATT_HEADS = ATT_WIDTH // ATT_HEAD_DIM
LEFT_CHUNKS = 8
BAND_CHUNKS = LEFT_CHUNKS + 1
MAX_REL = 256
REL_SIZE = MAX_REL + CHUNK
ROPE_BASE = 10000.0
N_EXPERTS = 32
TOP_K = 4
D_FF = D_MODEL
SWIGLU_LIMIT = 7.0
SWIGLU_ALPHA = 1.702
EXPERT_BLOCK = 256
NORM_EPS = 1e-6
IN_COLS = 4 * RET_WIDTH + 3 * ATT_WIDTH
NEG_INF = -1e30

kernel_name = 'hybrid_retention_chunkattn_moe_block'


def rms_norm(x, gain):
    x32 = x.astype(jnp.float32)
    y = x32 * lax.rsqrt(jnp.mean(x32 * x32, axis=-1, keepdims=True) + NORM_EPS)
    return (y * gain.astype(jnp.float32)).astype(x.dtype)


def rotary(t, positions):
    half = t.shape[-1] // 2
    inv_freq = 1.0 / (ROPE_BASE ** (jnp.arange(half, dtype=jnp.float32) / half))
    ang = positions.astype(jnp.float32)[..., None] * inv_freq
    cos = jnp.cos(ang)[:, :, None, :].astype(t.dtype)
    sin = jnp.sin(ang)[:, :, None, :].astype(t.dtype)
    t1, t2 = t[..., :half], t[..., half:]
    return jnp.concatenate([t1 * cos - t2 * sin, t1 * sin + t2 * cos], axis=-1)


def retention(q, k, v, g, positions):
    B, S, _ = q.shape
    L = CHUNK
    NC = S // L
    dt = q.dtype
    q = rotary(q.reshape(B, S, RET_HEADS, RET_HEAD_DIM), positions)
    k = rotary(k.reshape(B, S, RET_HEADS, RET_HEAD_DIM), positions) * (RET_HEAD_DIM ** -0.5)
    v = v.reshape(B, S, RET_HEADS, RET_HEAD_DIM)
    log_gamma = jnp.log(1.0 - jnp.exp(jnp.linspace(math.log(1.0 / 32), math.log(1.0 / 512), RET_HEADS)))
    n = jnp.arange(L, dtype=jnp.float32)
    d_intra = jnp.exp(jnp.abs(n[:, None] - n[None, :])[None] * log_gamma[:, None, None]).astype(dt)
    d_q = jnp.exp((n[:, None] + 1.0) * log_gamma[None, :]).astype(dt)
    d_k = jnp.exp((L - 1.0 - n)[:, None] * log_gamma[None, :]).astype(dt)
    d_chunk = jnp.exp(L * log_gamma).astype(dt)
    qc = q.reshape(B, NC, L, RET_HEADS, RET_HEAD_DIM)
    kc = k.reshape(B, NC, L, RET_HEADS, RET_HEAD_DIM)
    vc = v.reshape(B, NC, L, RET_HEADS, RET_HEAD_DIM)
    s = jnp.einsum('bcnhd,bcmhd->bchnm', qc, kc) * d_intra[None, None]
    o = jnp.einsum('bchnm,bcmhe->bcnhe', s, vc)
    kv = jnp.einsum('bcmhd,bcmhe->bchde', kc * d_k[None, None, :, :, None], vc)

    def step(state, kv_c):
        return state * d_chunk[None, :, None, None] + kv_c, state

    _, s_prev = lax.scan(step, jnp.zeros_like(kv[:, 0]), jnp.moveaxis(kv, 1, 0))
    s_prev = jnp.moveaxis(s_prev, 0, 1)
    o = o + jnp.einsum('bcnhd,bchde->bcnhe', qc * d_q[None, None, :, :, None], s_prev)
    o32 = o.astype(jnp.float32)
    o32 = o32 * lax.rsqrt(jnp.mean(o32 * o32, axis=-1, keepdims=True) + NORM_EPS)
    return o32.reshape(B, S, RET_WIDTH).astype(dt) * jax.nn.silu(g)


def chunked_attention(q, k, v, rel_bias):
    B, S, _ = q.shape
    L = CHUNK
    NC = S // L
    q = q.reshape(B, NC, L, ATT_HEADS, ATT_HEAD_DIM)
    pad = ((0, 0), (LEFT_CHUNKS * L, 0), (0, 0))
    kc = jnp.pad(k, pad).reshape(B, NC + LEFT_CHUNKS, L, ATT_HEADS, ATT_HEAD_DIM)
    vc = jnp.pad(v, pad).reshape(B, NC + LEFT_CHUNKS, L, ATT_HEADS, ATT_HEAD_DIM)
    band = jnp.arange(NC)[:, None] + jnp.arange(BAND_CHUNKS)[None, :]
    kb = kc[:, band].reshape(B, NC, BAND_CHUNKS * L, ATT_HEADS, ATT_HEAD_DIM)
    vb = vc[:, band].reshape(B, NC, BAND_CHUNKS * L, ATT_HEADS, ATT_HEAD_DIM)
    valid = jnp.repeat(band >= LEFT_CHUNKS, L, axis=1)
    n = jnp.arange(L)
    w = jnp.arange(BAND_CHUNKS)
    rel = (LEFT_CHUNKS - w)[None, :, None] * L + n[:, None, None] - n[None, None, :]
    idx = jnp.clip(rel, -(L - 1), MAX_REL) + (L - 1)
    bias = rel_bias[:, idx.reshape(L, BAND_CHUNKS * L)].astype(jnp.float32)
    s = jnp.einsum('bcnhd,bckhd->bhcnk', q, kb).astype(jnp.float32) * (ATT_HEAD_DIM ** -0.5) + bias[None, :, None]
    s = jnp.where(valid[None, None, :, None, :], s, NEG_INF)
    p = jax.nn.softmax(s, axis=-1).astype(v.dtype)
    o = jnp.einsum('bhcnk,bckhd->bcnhd', p, vb)
    return o.reshape(B, S, ATT_WIDTH)


def moe(xn, router_w, router_b, w_gate, b_gate, w_up, b_up, w_down, b_down):
    T, D = xn.shape
    logits = (xn @ router_w + router_b).astype(jnp.float32)
    top_v, top_i = lax.top_k(logits, TOP_K)
    gates = jax.nn.softmax(top_v, axis=-1)
    A = T * TOP_K
    e = top_i.reshape(A)
    tok = jnp.repeat(jnp.arange(T, dtype=jnp.int32), TOP_K)
    wts = gates.reshape(A)
    order = jnp.argsort(e)
    se, stok, sw = e[order], tok[order], wts[order]
    counts = jnp.zeros((N_EXPERTS,), jnp.int32).at[e].add(1)
    starts = jnp.cumsum(counts) - counts
    padded = ((counts + EXPERT_BLOCK - 1) // EXPERT_BLOCK) * EXPERT_BLOCK
    pends = jnp.cumsum(padded)
    pstarts = pends - padded
    dest = pstarts[se] + jnp.arange(A, dtype=jnp.int32) - starts[se]
    NB = A // EXPERT_BLOCK + N_EXPERTS
    P = NB * EXPERT_BLOCK
    buf_tok = jnp.full((P,), T, jnp.int32).at[dest].set(stok)
    buf_w = jnp.zeros((P,), jnp.float32).at[dest].set(sw)
    blk_e = jnp.minimum(jnp.searchsorted(pends, jnp.arange(NB, dtype=jnp.int32) * EXPERT_BLOCK, side='right'), N_EXPERTS - 1)
    x_pad = jnp.concatenate([xn, jnp.zeros((1, D), xn.dtype)], axis=0)

    def expert_block(args):
        tok_b, w_b, e_b = args
        xb = x_pad[tok_b]
        gate = jnp.minimum(xb @ w_gate[e_b] + b_gate[e_b], SWIGLU_LIMIT)
        up = jnp.clip(xb @ w_up[e_b] + b_up[e_b], -SWIGLU_LIMIT, SWIGLU_LIMIT)
        hid = (up + 1.0) * gate * jax.nn.sigmoid(SWIGLU_ALPHA * gate)
        return (hid @ w_down[e_b] + b_down[e_b]) * w_b[:, None]

    outs = lax.map(expert_block, (buf_tok.reshape(NB, EXPERT_BLOCK), buf_w.reshape(NB, EXPERT_BLOCK).astype(xn.dtype), blk_e))
    y = jnp.zeros((T + 1, D), xn.dtype).at[buf_tok].add(outs.reshape(P, D))
    return y[:T]


def setup_inputs(seed: int = 0) -> dict:
    key = jax.random.key(seed)
    ks = jax.random.split(key, 18)
    f32 = jnp.float32
    x = jax.random.normal(ks[0], (BATCH, SEQ, D_MODEL), f32)
    start = jax.random.randint(ks[1], (BATCH, 1), 0, 100000, dtype=jnp.int32)
    positions = (start + jnp.arange(SEQ, dtype=jnp.int32)[None, :]).astype(jnp.int32)
    norm_mix_g = 1.0 + 0.02 * jax.random.normal(ks[2], (DEPTH, D_MODEL), f32)
    w_in = jax.random.normal(ks[3], (DEPTH, D_MODEL, IN_COLS), f32) * D_MODEL ** -0.5
    w_out = jax.random.normal(ks[4], (DEPTH, MIX_WIDTH, D_MODEL), f32) * MIX_WIDTH ** -0.5
    rel_bias = 0.1 * jax.random.normal(ks[5], (DEPTH, ATT_HEADS, REL_SIZE), f32)
    norm_ffn_g = 1.0 + 0.02 * jax.random.normal(ks[6], (DEPTH, D_MODEL), f32)
    router_w = jax.random.normal(ks[7], (DEPTH, D_MODEL, N_EXPERTS), f32) * D_MODEL ** -0.5
    router_b = 0.01 * jax.random.normal(ks[8], (DEPTH, N_EXPERTS), f32)
    w_gate = jax.random.normal(ks[9], (DEPTH, N_EXPERTS, D_MODEL, D_FF), f32) * D_MODEL ** -0.5
    b_gate = 0.01 * jax.random.normal(ks[10], (DEPTH, N_EXPERTS, D_FF), f32)
    w_up = jax.random.normal(ks[11], (DEPTH, N_EXPERTS, D_MODEL, D_FF), f32) * D_MODEL ** -0.5
    b_up = 0.01 * jax.random.normal(ks[12], (DEPTH, N_EXPERTS, D_FF), f32)
    w_down = jax.random.normal(ks[13], (DEPTH, N_EXPERTS, D_FF, D_MODEL), f32) * D_FF ** -0.5
    b_down = 0.01 * jax.random.normal(ks[14], (DEPTH, N_EXPERTS, D_MODEL), f32)
    norm_final_g = 1.0 + 0.02 * jax.random.normal(ks[15], (D_MODEL,), f32)
    return {'x': x, 'positions': positions, 'norm_mix_g': norm_mix_g, 'w_in': w_in, 'w_out': w_out,
            'rel_bias': rel_bias, 'norm_ffn_g': norm_ffn_g, 'router_w': router_w, 'router_b': router_b,
            'w_gate': w_gate, 'b_gate': b_gate, 'w_up': w_up, 'b_up': b_up, 'w_down': w_down,
            'b_down': b_down, 'norm_final_g': norm_final_g}


def reference(x, positions, norm_mix_g, w_in, w_out, rel_bias, norm_ffn_g, router_w, router_b,
              w_gate, b_gate, w_up, b_up, w_down, b_down, norm_final_g):
    B, S, D = x.shape
    splits = [RET_WIDTH, 2 * RET_WIDTH, 3 * RET_WIDTH, 4 * RET_WIDTH,
              4 * RET_WIDTH + ATT_WIDTH, 4 * RET_WIDTH + 2 * ATT_WIDTH]
    h = x
    for l in range(DEPTH):
        hn = rms_norm(h, norm_mix_g[l])
        proj = hn @ w_in[l]
        rq, rk, rv, rg, aq, ak, av = jnp.split(proj, splits, axis=-1)
        ret = retention(rq, rk, rv, rg, positions)
        att = chunked_attention(aq, ak, av, rel_bias[l])
        h = h + jnp.concatenate([ret, att], axis=-1) @ w_out[l]
        hn = rms_norm(h, norm_ffn_g[l])
        y = moe(hn.reshape(B * S, D), router_w[l], router_b[l], w_gate[l], b_gate[l],
                w_up[l], b_up[l], w_down[l], b_down[l])
        h = h + y.reshape(B, S, D)
    return rms_norm(h, norm_final_g)
```

```python
import functools
import math

import jax
import jax.numpy as jnp
import numpy as np
from jax import lax
from jax.experimental import pallas as pl
from jax.experimental.pallas import tpu as pltpu

D_MODEL = 2048
CHUNK = 64
RET_WIDTH = 1024
ATT_WIDTH = 1024
RET_HEAD_DIM = 256
RET_HEADS = RET_WIDTH // RET_HEAD_DIM
ATT_HEAD_DIM = 128
ATT_HEADS = ATT_WIDTH // ATT_HEAD_DIM
LEFT_CHUNKS = 8
MAX_REL = 256
REL_SIZE = MAX_REL + CHUNK
ROPE_BASE = 10000.0
N_EXPERTS = 32
TOP_K = 4
D_FF = D_MODEL
SWIGLU_LIMIT = 7.0
SWIGLU_ALPHA = 1.702
NORM_EPS = 1e-6
IN_COLS = 4 * RET_WIDTH + 3 * ATT_WIDTH
NEG_INF = -1e30

BF16 = jnp.bfloat16
F32 = jnp.float32

VMEM_LIMIT_BYTES = 56 * 1024 * 1024

ATT_QBLK = 256
ATT_LEFT = LEFT_CHUNKS * CHUNK
ATT_WIN = ATT_LEFT + ATT_QBLK
ROLL_W = 1024

ROW_GROUP = 1024
SUB_ROWS = 256
SUB_SHIFT = SUB_ROWS.bit_length() - 1
assert 1 << SUB_SHIFT == SUB_ROWS
FF_TILE = 256
LANES = 128
PACK_W = D_MODEL // 2
PACK_C = PACK_W // LANES
ROW_C = D_MODEL // LANES


def _cparams(sem):
    return pltpu.CompilerParams(dimension_semantics=sem, vmem_limit_bytes=VMEM_LIMIT_BYTES)


def _in_proj_kernel(x_ref, g_ref, w_ref, o_ref, hn_ref):
    @pl.when(pl.program_id(1) == 0)
    def _():
        x = x_ref[...]
        ms = jnp.mean(x * x, axis=-1, keepdims=True)
        hn_ref[...] = (x * lax.rsqrt(ms + NORM_EPS) * g_ref[...]).astype(BF16)

    o_ref[...] = jnp.dot(hn_ref[...], w_ref[...], preferred_element_type=F32).astype(o_ref.dtype)


def _in_proj(x2, g, w_bf):
    T = x2.shape[0]
    tm, tn = 1024, 1024
    return pl.pallas_call(
        _in_proj_kernel,
        grid=(T // tm, IN_COLS // tn),
        in_specs=[
            pl.BlockSpec((tm, D_MODEL), lambda i, j: (i, 0)),
            pl.BlockSpec((1, D_MODEL), lambda i, j: (0, 0)),
            pl.BlockSpec((D_MODEL, tn), lambda i, j: (0, j)),
        ],
        out_specs=pl.BlockSpec((tm, tn), lambda i, j: (i, j)),
        out_shape=jax.ShapeDtypeStruct((T, IN_COLS), BF16),
        scratch_shapes=[pltpu.VMEM((tm, D_MODEL), BF16)],
        compiler_params=_cparams(("parallel", "arbitrary")),
        name="in_proj",
    )(x2, g, w_bf)


def _bias_table_kernel(sel_ref, rb_ref, o_ref):
    rb = rb_ref[...]
    hi = rb.astype(BF16)
    r1 = rb - hi.astype(F32)
    mid = r1.astype(BF16)
    lo = (r1 - mid.astype(F32)).astype(BF16)
    sel = sel_ref[...]
    frow = (jnp.dot(hi, sel, preferred_element_type=F32)
            + jnp.dot(mid, sel, preferred_element_type=F32)
            + jnp.dot(lo, sel, preferred_element_type=F32))
    i = lax.broadcasted_iota(jnp.int32, (ATT_QBLK, ATT_WIN), 0)
    c = lax.broadcasted_iota(jnp.int32, (ATT_QBLK, ATT_WIN), 1)
    qc = (i + ATT_LEFT) // CHUNK
    kc = c // CHUNK
    valid = (kc <= qc) & (kc >= qc - LEFT_CHUNKS)
    for h in range(ATT_HEADS):
        rows = jnp.broadcast_to(frow[h:h + 1, :], (ATT_QBLK, ROLL_W))
        rolled = pltpu.roll(rows, 0, 1, stride=1, stride_axis=0)
        o_ref[h] = jnp.where(valid, rolled[:, :ATT_WIN], NEG_INF)


def _bias_table(rel_bias):
    u = np.arange(-(ATT_QBLK - 1), ATT_WIN)
    idx = np.clip(ATT_LEFT - u, -(CHUNK - 1), MAX_REL) + (CHUNK - 1)
    sel = np.zeros((REL_SIZE, ROLL_W), np.float32)
    sel[idx, u % ROLL_W] = 1.0
    sel = jnp.asarray(sel, BF16)
    return pl.pallas_call(
        _bias_table_kernel,
        out_shape=jax.ShapeDtypeStruct((ATT_HEADS, ATT_QBLK, ATT_WIN), F32),
        compiler_params=pltpu.CompilerParams(vmem_limit_bytes=VMEM_LIMIT_BYTES),
        name="bias_table",
    )(sel, rel_bias)


RET_RB = 256


def _retention_kernel(pos_ref, invf_ref, q_ref, k_ref, v_ref, g_ref,
                      dintra_ref, dq_ref, dk_ref, dchunk_ref, o_ref, state_ref):
    @pl.when(pl.program_id(1) == 0)
    def _():
        state_ref[...] = jnp.zeros_like(state_ref)

    half = RET_HEAD_DIM // 2
    ang = pos_ref[...].astype(F32) * invf_ref[...]
    cos = jnp.cos(ang)
    sin = jnp.sin(ang)
    kscale = RET_HEAD_DIM ** -0.5

    for h in range(RET_HEADS):
        c0 = h * RET_HEAD_DIM
        q = q_ref[:, c0:c0 + RET_HEAD_DIM].astype(F32)
        k = k_ref[:, c0:c0 + RET_HEAD_DIM].astype(F32)
        q1, q2 = q[:, :half], q[:, half:]
        k1, k2 = k[:, :half], k[:, half:]
        qr = jnp.concatenate([q1 * cos - q2 * sin, q1 * sin + q2 * cos], axis=-1)
        kr = jnp.concatenate([k1 * cos - k2 * sin, k1 * sin + k2 * cos], axis=-1) * kscale
        d_intra = dintra_ref[h]
        d_q = dq_ref[h]
        d_k = dk_ref[h]
        d_c = dchunk_ref[h]
        for c in range(RET_RB // CHUNK):
            r0 = c * CHUNK
            qc = qr[r0:r0 + CHUNK]
            kc = kr[r0:r0 + CHUNK]
            vc = v_ref[r0:r0 + CHUNK, c0:c0 + RET_HEAD_DIM]
            qb = qc.astype(BF16)
            s = lax.dot_general(qb, kc.astype(BF16), (((1,), (1,)), ((), ())),
                                preferred_element_type=F32) * d_intra
            o = jnp.dot(s.astype(BF16), vc, preferred_element_type=F32)
            st = state_ref[h]
            o = o + jnp.dot((qc * d_q).astype(BF16), st.astype(BF16), preferred_element_type=F32)
            kv = lax.dot_general((kc * d_k).astype(BF16), vc, (((0,), (0,)), ((), ())),
                                 preferred_element_type=F32)
            state_ref[h] = st * d_c + kv
            o = o * lax.rsqrt(jnp.mean(o * o, axis=-1, keepdims=True) + NORM_EPS)
            g = g_ref[r0:r0 + CHUNK, c0:c0 + RET_HEAD_DIM].astype(F32)
            o_ref[r0:r0 + CHUNK, c0:c0 + RET_HEAD_DIM] = (o * (g * jax.nn.sigmoid(g))).astype(o_ref.dtype)


def _retention(proj, pos2, B, S):
    T = B * S
    nb = S // RET_RB
    half = RET_HEAD_DIM // 2
    L = CHUNK
    inv_freq = (1.0 / (ROPE_BASE ** (jnp.arange(half, dtype=F32) / half))).reshape(1, half)
    log_gamma = jnp.log(1.0 - jnp.exp(jnp.linspace(math.log(1.0 / 32), math.log(1.0 / 512), RET_HEADS)))
    n = jnp.arange(L, dtype=F32)
    d_intra = jnp.exp(jnp.abs(n[:, None] - n[None, :])[None] * log_gamma[:, None, None]).astype(F32)
    d_q = jnp.exp((n[None, :] + 1.0) * log_gamma[:, None]).astype(F32)[:, :, None]
    d_k = jnp.exp((L - 1.0 - n)[None, :] * log_gamma[:, None]).astype(F32)[:, :, None]
    d_chunk = jnp.exp(L * log_gamma).astype(F32).reshape(RET_HEADS, 1, 1)

    def col(j):
        return pl.BlockSpec((RET_RB, RET_WIDTH), lambda b, c: (b * nb + c, j))

    def full(shape):
        return pl.BlockSpec(shape, lambda b, c: (0,) * len(shape))

    return pl.pallas_call(
        _retention_kernel,
        grid=(B, nb),
        in_specs=[
            pl.BlockSpec((RET_RB, 1), lambda b, c: (b * nb + c, 0)),
            full((1, half)),
            col(0), col(1), col(2), col(3),
            full((RET_HEADS, L, L)), full((RET_HEADS, L, 1)), full((RET_HEADS, L, 1)),
            full((RET_HEADS, 1, 1)),
        ],
        out_specs=pl.BlockSpec((RET_RB, RET_WIDTH), lambda b, c: (b * nb + c, 0)),
        out_shape=jax.ShapeDtypeStruct((T, RET_WIDTH), BF16),
        scratch_shapes=[pltpu.VMEM((RET_HEADS, RET_HEAD_DIM, RET_HEAD_DIM), F32)],
        compiler_params=_cparams(("parallel", "arbitrary")),
        name="retention",
    )(pos2, inv_freq, proj, proj, proj, proj, d_intra, d_q, d_k, d_chunk)


def _attention_kernel(q_ref, k_ref, v_ref, tab_ref, o_ref):
    qb = pl.program_id(1)
    scale = ATT_HEAD_DIM ** -0.5

    def run(k_start, n_keys, col0):
        for h in range(ATT_HEADS):
            c0 = h * ATT_HEAD_DIM
            q = q_ref[:, c0:c0 + ATT_HEAD_DIM]
            k = k_ref[pl.ds(k_start, n_keys), c0:c0 + ATT_HEAD_DIM]
            v = v_ref[pl.ds(k_start, n_keys), c0:c0 + ATT_HEAD_DIM]
            s = lax.dot_general(q, k, (((1,), (1,)), ((), ())), preferred_element_type=F32)
            s = s * scale + tab_ref[h, :, col0:col0 + n_keys]
            m = jnp.max(s, axis=-1, keepdims=True)
            e = jnp.exp(s - m)
            p = e / jnp.sum(e, axis=-1, keepdims=True)
            o = jnp.dot(p.astype(BF16), v, preferred_element_type=F32)
            o_ref[:, c0:c0 + ATT_HEAD_DIM] = o.astype(o_ref.dtype)

    @pl.when(qb == 0)
    def _():
        run(0, ATT_QBLK, ATT_LEFT)

    @pl.when(qb == 1)
    def _():
        run(0, 2 * ATT_QBLK, ATT_QBLK)

    @pl.when(qb >= 2)
    def _():
        run(pl.multiple_of((qb - 2) * ATT_QBLK, ATT_QBLK), ATT_WIN, 0)


def _attention(proj, table, B, S):
    T = B * S
    nq = S // ATT_QBLK
    return pl.pallas_call(
        _attention_kernel,
        grid=(B, nq),
        in_specs=[
            pl.BlockSpec((ATT_QBLK, ATT_WIDTH), lambda b, q: (b * nq + q, 4)),
            pl.BlockSpec((S, ATT_WIDTH), lambda b, q: (b, 5)),
            pl.BlockSpec((S, ATT_WIDTH), lambda b, q: (b, 6)),
            pl.BlockSpec((ATT_HEADS, ATT_QBLK, ATT_WIN), lambda b, q: (0, 0, 0)),
        ],
        out_specs=pl.BlockSpec((ATT_QBLK, ATT_WIDTH), lambda b, q: (b * nq + q, 0)),
        out_shape=jax.ShapeDtypeStruct((T, ATT_WIDTH), BF16),
        compiler_params=_cparams(("parallel", "arbitrary")),
        name="attention",
    )(proj, proj, proj, table)


OR_TM = 256


def _out_router_kernel(x_ref, ret_ref, att_ref, wo_ref, g_ref, rw_ref, rb_ref,
                       h_ref, hn_ref, ti_ref, gt_ref, rk_ref, cnt_ref, carry_ref):
    @pl.when(pl.program_id(0) == 0)
    def _():
        carry_ref[...] = jnp.zeros_like(carry_ref)

    mix = (jnp.dot(ret_ref[...], wo_ref[:RET_WIDTH, :], preferred_element_type=F32)
           + jnp.dot(att_ref[...], wo_ref[RET_WIDTH:, :], preferred_element_type=F32))
    h = x_ref[...] + mix
    h_ref[...] = h
    hn = h * lax.rsqrt(jnp.mean(h * h, axis=-1, keepdims=True) + NORM_EPS) * g_ref[...]
    hn_bf = hn.astype(BF16)
    bits = lax.bitcast_convert_type(hn_bf.astype(F32), jnp.uint32)
    packed = bits[:, :PACK_W] | (bits[:, PACK_W:] >> 16)
    for c in range(PACK_C):
        hn_ref[:, c, :] = packed[:, c * LANES:(c + 1) * LANES]

    hn_lo = (hn - hn_bf.astype(F32)).astype(BF16)
    rw = rw_ref[...]
    rw_hi = rw.astype(BF16)
    rw_lo = (rw - rw_hi.astype(F32)).astype(BF16)
    logits = (jnp.dot(hn_bf, rw_hi, preferred_element_type=F32)
              + jnp.dot(hn_bf, rw_lo, preferred_element_type=F32)
              + jnp.dot(hn_lo, rw_hi, preferred_element_type=F32)) + rb_ref[...]

    lane = lax.broadcasted_iota(jnp.int32, (OR_TM, N_EXPERTS), 1)
    work = logits
    vals, idxs = [], []
    for _ in range(TOP_K):
        m = jnp.max(work, axis=-1, keepdims=True)
        idx = jnp.min(jnp.where(work == m, lane, N_EXPERTS), axis=-1, keepdims=True)
        vals.append(m)
        idxs.append(idx)
        work = jnp.where(lane == idx, -jnp.inf, work)
    es = [jnp.exp(v - vals[0]) for v in vals]
    den = es[0] + es[1] + es[2] + es[3]

    sel = jnp.zeros((OR_TM, N_EXPERTS), F32)
    for idx in idxs:
        sel = sel + (lane == idx).astype(F32)
    r = lax.broadcasted_iota(jnp.int32, (OR_TM, OR_TM), 0)
    cc = lax.broadcasted_iota(jnp.int32, (OR_TM, OR_TM), 1)
    tri = (cc < r).astype(BF16)
    cum = jnp.dot(tri, sel.astype(BF16), preferred_element_type=F32) + carry_ref[...]
    for kk in range(TOP_K):
        rank = jnp.sum(jnp.where(lane == idxs[kk], cum, 0.0), axis=-1, keepdims=True)
        rk_ref[:, kk:kk + 1] = rank.astype(jnp.int32)
        ti_ref[:, kk:kk + 1] = idxs[kk]
        gt_ref[:, kk:kk + 1] = es[kk] / den
    carry_ref[...] = carry_ref[...] + jnp.sum(sel, axis=0, keepdims=True)
    cnt_ref[...] = carry_ref[...].astype(jnp.int32)


def _out_router(x2, ret, att, wo_bf, g, rw, rb):
    T = x2.shape[0]
    tm = OR_TM
    row = lambda n: pl.BlockSpec((tm, n), lambda i: (i, 0))
    fix = lambda a, b: pl.BlockSpec((a, b), lambda i: (0, 0))
    return pl.pallas_call(
        _out_router_kernel,
        grid=(T // tm,),
        in_specs=[row(D_MODEL), row(RET_WIDTH), row(ATT_WIDTH), fix(D_MODEL, D_MODEL),
                  fix(1, D_MODEL), fix(D_MODEL, N_EXPERTS), fix(1, N_EXPERTS)],
        out_specs=[row(D_MODEL), pl.BlockSpec((tm, PACK_C, LANES), lambda i: (i, 0, 0)),
                   row(TOP_K), row(TOP_K), row(TOP_K), fix(1, N_EXPERTS)],
        out_shape=[
            jax.ShapeDtypeStruct((T, D_MODEL), F32),
            jax.ShapeDtypeStruct((T, PACK_C, LANES), jnp.uint32),
            jax.ShapeDtypeStruct((T, TOP_K), jnp.int32),
            jax.ShapeDtypeStruct((T, TOP_K), F32),
            jax.ShapeDtypeStruct((T, TOP_K), jnp.int32),
            jax.ShapeDtypeStruct((1, N_EXPERTS), jnp.int32),
        ],
        scratch_shapes=[pltpu.VMEM((1, N_EXPERTS), F32)],
        compiler_params=_cparams(("arbitrary",)),
        name="out_router",
    )(x2, ret, att, wo_bf, g, rw, rb)


def _experts_kernel(ge_ref, gblk_ref, grows_ref, tok_ref, yrow_ref, hn_hbm,
                    wg_ref, bg_ref, wu_ref, bu_ref, wd_ref, bd_ref, y_hbm,
                    xg, xbf, acc, wg_s, wu_s, wd_s, stage, sem_g, sem_s):
    g = pl.program_id(0)
    f = pl.program_id(1)
    nrows = grows_ref[g]
    nsub = lax.shift_right_logical(nrows + (SUB_ROWS - 1), SUB_SHIFT)

    def gather_copy(r):
        return pltpu.make_async_copy(hn_hbm.at[pl.ds(tok_ref[0, 0, r], 1)], xg.at[pl.ds(r, 1)], sem_g)

    def scatter_copy(r0, j):
        return pltpu.make_async_copy(stage.at[pl.ds(j, 1)], y_hbm.at[pl.ds(yrow_ref[0, 0, r0 + j], 1)], sem_s)

    @pl.when((nsub > 0) & (f == 0))
    def _():
        last0 = pl.multiple_of((nsub - 1) * SUB_ROWS, SUB_ROWS)
        xg[pl.ds(last0, SUB_ROWS), :, :] = jnp.zeros((SUB_ROWS, PACK_C, LANES), jnp.uint32)

        def start(r, _):
            gather_copy(r).start()
            return 0

        def wait(r, _):
            gather_copy(r).wait()
            return 0

        lax.fori_loop(0, nrows, start, 0)
        lax.fori_loop(0, nrows, wait, 0)

        def unpack(s, _):
            r0 = pl.multiple_of(s * SUB_ROWS, SUB_ROWS)
            for c in range(PACK_C):
                w = xg[pl.ds(r0, SUB_ROWS), c, :]
                hi = lax.bitcast_convert_type(w & jnp.uint32(0xFFFF0000), F32)
                lo = lax.bitcast_convert_type(w << 16, F32)
                xbf[pl.ds(r0, SUB_ROWS), c * LANES:(c + 1) * LANES] = hi.astype(BF16)
                xbf[pl.ds(r0, SUB_ROWS), PACK_W + c * LANES:PACK_W + (c + 1) * LANES] = lo.astype(BF16)
            acc[pl.ds(r0, SUB_ROWS), :] = jnp.broadcast_to(bd_ref[0], (SUB_ROWS, D_MODEL))
            return 0

        lax.fori_loop(0, nsub, unpack, 0)

    @pl.when(nsub > 0)
    def _():
        wg_s[...] = wg_ref[0].astype(BF16)
        wu_s[...] = wu_ref[0].astype(BF16)
        wd_s[...] = wd_ref[0].astype(BF16)
        bg = bg_ref[0]
        bu = bu_ref[0]

        def body(s, _):
            r0 = pl.multiple_of(s * SUB_ROWS, SUB_ROWS)
            xb = xbf[pl.ds(r0, SUB_ROWS), :]
            gate = jnp.minimum(jnp.dot(xb, wg_s[...], preferred_element_type=F32) + bg, SWIGLU_LIMIT)
            up = jnp.clip(jnp.dot(xb, wu_s[...], preferred_element_type=F32) + bu, -SWIGLU_LIMIT, SWIGLU_LIMIT)
            hid = (up + 1.0) * gate * jax.nn.sigmoid(SWIGLU_ALPHA * gate)
            acc[pl.ds(r0, SUB_ROWS), :] += jnp.dot(hid.astype(BF16), wd_s[...], preferred_element_type=F32)
            return 0

        lax.fori_loop(0, nsub, body, 0)

    @pl.when((nsub > 0) & (f == pl.num_programs(1) - 1))
    def _():
        def emit(s, _):
            r0 = pl.multiple_of(s * SUB_ROWS, SUB_ROWS)
            for c in range(ROW_C):
                stage[:, c, :] = acc[pl.ds(r0, SUB_ROWS), c * LANES:(c + 1) * LANES]

            def start(j, _):
                scatter_copy(r0, j).start()
                return 0

            def wait(j, _):
                scatter_copy(r0, j).wait()
                return 0

            n_here = jnp.minimum(nrows - r0, SUB_ROWS)
            lax.fori_loop(0, n_here, start, 0)
            lax.fori_loop(0, n_here, wait, 0)
            return 0

        lax.fori_loop(0, nsub, emit, 0)


def _experts(g_e, g_blk, g_rows, tok, yrow, hn2p, w_gate, b_gate, w_up, b_up, w_down, b_down, n_groups, n_yrows):
    nf = D_FF // FF_TILE
    bg3 = b_gate.reshape(N_EXPERTS, 1, D_FF)
    bu3 = b_up.reshape(N_EXPERTS, 1, D_FF)
    bd3 = b_down.reshape(N_EXPERTS, 1, D_MODEL)
    tok3 = tok.reshape(n_groups, 1, ROW_GROUP)
    yrow3 = yrow.reshape(n_groups, 1, ROW_GROUP)

    def ff(gi, fi, ge, gb, gn):
        return jnp.where(gn[gi] > 0, fi, nf - 1)

    grid_spec = pltpu.PrefetchScalarGridSpec(
        num_scalar_prefetch=3,
        grid=(n_groups, nf),
        in_specs=[
            pl.BlockSpec((1, 1, ROW_GROUP), lambda gi, fi, ge, gb, gn: (gb[gi], 0, 0), memory_space=pltpu.SMEM),
            pl.BlockSpec((1, 1, ROW_GROUP), lambda gi, fi, ge, gb, gn: (gb[gi], 0, 0), memory_space=pltpu.SMEM),
            pl.BlockSpec(memory_space=pl.ANY),
            pl.BlockSpec((1, D_MODEL, FF_TILE), lambda gi, fi, ge, gb, gn: (ge[gi], 0, ff(gi, fi, ge, gb, gn))),
            pl.BlockSpec((1, 1, FF_TILE), lambda gi, fi, ge, gb, gn: (ge[gi], 0, ff(gi, fi, ge, gb, gn))),
            pl.BlockSpec((1, D_MODEL, FF_TILE), lambda gi, fi, ge, gb, gn: (ge[gi], 0, ff(gi, fi, ge, gb, gn))),
            pl.BlockSpec((1, 1, FF_TILE), lambda gi, fi, ge, gb, gn: (ge[gi], 0, ff(gi, fi, ge, gb, gn))),
            pl.BlockSpec((1, FF_TILE, D_MODEL), lambda gi, fi, ge, gb, gn: (ge[gi], ff(gi, fi, ge, gb, gn), 0)),
            pl.BlockSpec((1, 1, D_MODEL), lambda gi, fi, ge, gb, gn: (ge[gi], 0, 0)),
        ],
        out_specs=pl.BlockSpec(memory_space=pl.ANY),
        scratch_shapes=[
            pltpu.VMEM((ROW_GROUP, PACK_C, LANES), jnp.uint32),
            pltpu.VMEM((ROW_GROUP, D_MODEL), BF16),
            pltpu.VMEM((ROW_GROUP, D_MODEL), F32),
            pltpu.VMEM((D_MODEL, FF_TILE), BF16), pltpu.VMEM((D_MODEL, FF_TILE), BF16),
            pltpu.VMEM((FF_TILE, D_MODEL), BF16),
            pltpu.VMEM((SUB_ROWS, ROW_C, LANES), F32),
            pltpu.SemaphoreType.DMA(()), pltpu.SemaphoreType.DMA(()),
        ],
    )
    return pl.pallas_call(
        _experts_kernel,
        grid_spec=grid_spec,
        out_shape=jax.ShapeDtypeStruct((n_yrows, ROW_C, LANES), F32),
        compiler_params=_cparams(("arbitrary", "arbitrary")),
        name="experts",
    )(g_e, g_blk, g_rows, tok3, yrow3, hn2p, w_gate, bg3, w_up, bu3, w_down, bd3)


CB_TM = 256


def _combine_kernel(h_ref, gt_ref, g_ref, y0_ref, y1_ref, y2_ref, y3_ref, o_ref, h2_ref):
    gt = gt_ref[...]
    ys = (y0_ref, y1_ref, y2_ref, y3_ref)
    for c in range(ROW_C):
        y = ys[0][:, c, :] * gt[:, 0:1]
        for kk in range(1, TOP_K):
            y = y + ys[kk][:, c, :] * gt[:, kk:kk + 1]
        h2_ref[:, c * LANES:(c + 1) * LANES] = h_ref[:, c * LANES:(c + 1) * LANES] + y
    h = h2_ref[...]
    o_ref[...] = h * lax.rsqrt(jnp.mean(h * h, axis=-1, keepdims=True) + NORM_EPS) * g_ref[...]


def _combine(h, gates, g_final, ybuf):
    T = h.shape[0]
    steps = T // CB_TM

    def yspec(kk):
        return pl.BlockSpec((CB_TM, ROW_C, LANES), lambda i: (kk * steps + i, 0, 0))

    return pl.pallas_call(
        _combine_kernel,
        grid=(steps,),
        in_specs=[
            pl.BlockSpec((CB_TM, D_MODEL), lambda i: (i, 0)),
            pl.BlockSpec((CB_TM, TOP_K), lambda i: (i, 0)),
            pl.BlockSpec((1, D_MODEL), lambda i: (0, 0)),
            yspec(0), yspec(1), yspec(2), yspec(3),
        ],
        out_specs=pl.BlockSpec((CB_TM, D_MODEL), lambda i: (i, 0)),
        out_shape=jax.ShapeDtypeStruct((T, D_MODEL), F32),
        scratch_shapes=[pltpu.VMEM((CB_TM, D_MODEL), F32)],
        compiler_params=_cparams(("parallel",)),
        name="combine",
    )(h, gates, g_final, ybuf, ybuf, ybuf, ybuf)


def _group_tables(counts, top_i, rank, n_groups, T):
    A = T * TOP_K
    P = n_groups * ROW_GROUP
    counts = counts.reshape(N_EXPERTS)
    groups_e = (counts + ROW_GROUP - 1) // ROW_GROUP
    gend = jnp.cumsum(groups_e)
    gstart = gend - groups_e
    total = gend[-1]
    dest = ((gstart * ROW_GROUP)[top_i] + rank).astype(jnp.int32)
    gidx = jnp.arange(n_groups, dtype=jnp.int32)
    used = gidx < total
    last = jnp.maximum(total - 1, 0)
    gsafe = jnp.where(used, gidx, last)
    g_e = jnp.minimum(jnp.searchsorted(gend, gsafe, side='right'), N_EXPERTS - 1).astype(jnp.int32)
    rows = jnp.clip(counts[g_e] - (gsafe - gstart[g_e]) * ROW_GROUP, 0, ROW_GROUP)
    g_rows = jnp.where(used, rows, 0).astype(jnp.int32)
    src = jnp.zeros((P,), jnp.int32).at[dest.reshape(A)].set(jnp.arange(A, dtype=jnp.int32))
    tok = src // TOP_K
    yrow = (src % TOP_K) * T + tok
    return g_e, gsafe.astype(jnp.int32), g_rows, tok, yrow


def kernel(x, positions, norm_mix_g, w_in, w_out, rel_bias, norm_ffn_g, router_w, router_b,
           w_gate, b_gate, w_up, b_up, w_down, b_down, norm_final_g):
    B, S, D = x.shape
    T = B * S
    depth = norm_mix_g.shape[0]
    assert depth == 1, "the combine kernel applies the final norm, so exactly one layer is supported"
    n_groups = (T * TOP_K) // ROW_GROUP + N_EXPERTS
    h = x.reshape(T, D)
    pos2 = positions.reshape(T, 1)
    for l in range(depth):
        proj = _in_proj(h, norm_mix_g[l].reshape(1, D), w_in[l].astype(BF16))
        table = _bias_table(rel_bias[l])
        ret = _retention(proj, pos2, B, S)
        att = _attention(proj, table, B, S)
        h, hn2, top_i, gates, rank, counts = _out_router(
            h, ret, att, w_out[l].astype(BF16), norm_ffn_g[l].reshape(1, D),
            router_w[l], router_b[l].reshape(1, N_EXPERTS))
        g_e, g_blk, g_rows, tok, yrow = _group_tables(counts, top_i, rank, n_groups, T)
        ybuf = _experts(g_e, g_blk, g_rows, tok, yrow, hn2, w_gate[l], b_gate[l], w_up[l], b_up[l],
                        w_down[l], b_down[l], n_groups, T * TOP_K)
        h = _combine(h, gates, norm_final_g.reshape(1, D), ybuf)
    return h.reshape(B, S, D)
```

```python
import functools
import math

import jax
import jax.numpy as jnp
import numpy as np
from jax import lax
from jax.experimental import pallas as pl
from jax.experimental.pallas import tpu as pltpu

D_MODEL = 2048
CHUNK = 64
RET_WIDTH = 1024
ATT_WIDTH = 1024
RET_HEAD_DIM = 256
RET_HEADS = RET_WIDTH // RET_HEAD_DIM
ATT_HEAD_DIM = 128
ATT_HEADS = ATT_WIDTH // ATT_HEAD_DIM
LEFT_CHUNKS = 8
MAX_REL = 256
REL_SIZE = MAX_REL + CHUNK
ROPE_BASE = 10000.0
N_EXPERTS = 32
TOP_K = 4
D_FF = D_MODEL
SWIGLU_LIMIT = 7.0
SWIGLU_ALPHA = 1.702
NORM_EPS = 1e-6
IN_COLS = 4 * RET_WIDTH + 3 * ATT_WIDTH
NEG_INF = -1e30

BF16 = jnp.bfloat16
F32 = jnp.float32

VMEM_LIMIT_BYTES = 56 * 1024 * 1024

ATT_QBLK = 256
ATT_LEFT = LEFT_CHUNKS * CHUNK
ATT_WIN = ATT_LEFT + ATT_QBLK
ROLL_W = 1024

ROW_GROUP = 1024
SUB_ROWS = 256
SUB_SHIFT = SUB_ROWS.bit_length() - 1
assert 1 << SUB_SHIFT == SUB_ROWS
FF_TILE = 256
LANES = 128
ROW_C = D_MODEL // LANES
GATHER_ROWS = ROW_GROUP // (D_FF // FF_TILE)


def _cparams(sem):
    return pltpu.CompilerParams(dimension_semantics=sem, vmem_limit_bytes=VMEM_LIMIT_BYTES)


def _in_proj_kernel(x_ref, g_ref, w_ref, o_ref, hn_ref):
    @pl.when(pl.program_id(1) == 0)
    def _():
        x = x_ref[...]
        ms = jnp.mean(x * x, axis=-1, keepdims=True)
        hn_ref[...] = (x * lax.rsqrt(ms + NORM_EPS) * g_ref[...]).astype(BF16)

    o_ref[...] = jnp.dot(hn_ref[...], w_ref[...], preferred_element_type=F32).astype(o_ref.dtype)


def _in_proj(x2, g, w_bf):
    T = x2.shape[0]
    tm, tn = 1024, 1024
    return pl.pallas_call(
        _in_proj_kernel,
        grid=(T // tm, IN_COLS // tn),
        in_specs=[
            pl.BlockSpec((tm, D_MODEL), lambda i, j: (i, 0)),
            pl.BlockSpec((1, D_MODEL), lambda i, j: (0, 0)),
            pl.BlockSpec((D_MODEL, tn), lambda i, j: (0, j)),
        ],
        out_specs=pl.BlockSpec((tm, tn), lambda i, j: (i, j)),
        out_shape=jax.ShapeDtypeStruct((T, IN_COLS), BF16),
        scratch_shapes=[pltpu.VMEM((tm, D_MODEL), BF16)],
        compiler_params=_cparams(("parallel", "arbitrary")),
        name="in_proj",
    )(x2, g, w_bf)


def _bias_table_kernel(sel_ref, rb_ref, o_ref):
    rb = rb_ref[...]
    hi = rb.astype(BF16)
    r1 = rb - hi.astype(F32)
    mid = r1.astype(BF16)
    lo = (r1 - mid.astype(F32)).astype(BF16)
    sel = sel_ref[...]
    frow = (jnp.dot(hi, sel, preferred_element_type=F32)
            + jnp.dot(mid, sel, preferred_element_type=F32)
            + jnp.dot(lo, sel, preferred_element_type=F32))
    i = lax.broadcasted_iota(jnp.int32, (ATT_QBLK, ATT_WIN), 0)
    c = lax.broadcasted_iota(jnp.int32, (ATT_QBLK, ATT_WIN), 1)
    qc = (i + ATT_LEFT) // CHUNK
    kc = c // CHUNK
    valid = (kc <= qc) & (kc >= qc - LEFT_CHUNKS)
    for h in range(ATT_HEADS):
        rows = jnp.broadcast_to(frow[h:h + 1, :], (ATT_QBLK, ROLL_W))
        rolled = pltpu.roll(rows, 0, 1, stride=1, stride_axis=0)
        o_ref[h] = jnp.where(valid, rolled[:, :ATT_WIN], NEG_INF)


def _bias_table(rel_bias):
    u = np.arange(-(ATT_QBLK - 1), ATT_WIN)
    idx = np.clip(ATT_LEFT - u, -(CHUNK - 1), MAX_REL) + (CHUNK - 1)
    sel = np.zeros((REL_SIZE, ROLL_W), np.float32)
    sel[idx, u % ROLL_W] = 1.0
    sel = jnp.asarray(sel, BF16)
    return pl.pallas_call(
        _bias_table_kernel,
        out_shape=jax.ShapeDtypeStruct((ATT_HEADS, ATT_QBLK, ATT_WIN), F32),
        compiler_params=pltpu.CompilerParams(vmem_limit_bytes=VMEM_LIMIT_BYTES),
        name="bias_table",
    )(sel, rel_bias)


RET_RB = 256


def _retention_kernel(pos_ref, invf_ref, q_ref, k_ref, v_ref, g_ref,
                      dintra_ref, dq_ref, dk_ref, dchunk_ref, o_ref, state_ref):
    @pl.when(pl.program_id(1) == 0)
    def _():
        state_ref[...] = jnp.zeros_like(state_ref)

    half = RET_HEAD_DIM // 2
    ang = pos_ref[...].astype(F32) * invf_ref[...]
    cos = jnp.cos(ang)
    sin = jnp.sin(ang)
    kscale = RET_HEAD_DIM ** -0.5

    for h in range(RET_HEADS):
        c0 = h * RET_HEAD_DIM
        q = q_ref[:, c0:c0 + RET_HEAD_DIM].astype(F32)
        k = k_ref[:, c0:c0 + RET_HEAD_DIM].astype(F32)
        q1, q2 = q[:, :half], q[:, half:]
        k1, k2 = k[:, :half], k[:, half:]
        qr = jnp.concatenate([q1 * cos - q2 * sin, q1 * sin + q2 * cos], axis=-1)
        kr = jnp.concatenate([k1 * cos - k2 * sin, k1 * sin + k2 * cos], axis=-1) * kscale
        d_intra = dintra_ref[h]
        d_q = dq_ref[h]
        d_k = dk_ref[h]
        d_c = dchunk_ref[h]
        for c in range(RET_RB // CHUNK):
            r0 = c * CHUNK
            qc = qr[r0:r0 + CHUNK]
            kc = kr[r0:r0 + CHUNK]
            vc = v_ref[r0:r0 + CHUNK, c0:c0 + RET_HEAD_DIM]
            qb = qc.astype(BF16)
            s = lax.dot_general(qb, kc.astype(BF16), (((1,), (1,)), ((), ())),
                                preferred_element_type=F32) * d_intra
            o = jnp.dot(s.astype(BF16), vc, preferred_element_type=F32)
            st = state_ref[h]
            o = o + jnp.dot((qc * d_q).astype(BF16), st.astype(BF16), preferred_element_type=F32)
            kv = lax.dot_general((kc * d_k).astype(BF16), vc, (((0,), (0,)), ((), ())),
                                 preferred_element_type=F32)
            state_ref[h] = st * d_c + kv
            o = o * lax.rsqrt(jnp.mean(o * o, axis=-1, keepdims=True) + NORM_EPS)
            g = g_ref[r0:r0 + CHUNK, c0:c0 + RET_HEAD_DIM].astype(F32)
            o_ref[r0:r0 + CHUNK, c0:c0 + RET_HEAD_DIM] = (o * (g * jax.nn.sigmoid(g))).astype(o_ref.dtype)


def _retention(proj, pos2, B, S):
    T = B * S
    nb = S // RET_RB
    half = RET_HEAD_DIM // 2
    L = CHUNK
    inv_freq = (1.0 / (ROPE_BASE ** (jnp.arange(half, dtype=F32) / half))).reshape(1, half)
    log_gamma = jnp.log(1.0 - jnp.exp(jnp.linspace(math.log(1.0 / 32), math.log(1.0 / 512), RET_HEADS)))
    n = jnp.arange(L, dtype=F32)
    d_intra = jnp.exp(jnp.abs(n[:, None] - n[None, :])[None] * log_gamma[:, None, None]).astype(F32)
    d_q = jnp.exp((n[None, :] + 1.0) * log_gamma[:, None]).astype(F32)[:, :, None]
    d_k = jnp.exp((L - 1.0 - n)[None, :] * log_gamma[:, None]).astype(F32)[:, :, None]
    d_chunk = jnp.exp(L * log_gamma).astype(F32).reshape(RET_HEADS, 1, 1)

    def col(j):
        return pl.BlockSpec((RET_RB, RET_WIDTH), lambda b, c: (b * nb + c, j))

    def full(shape):
        return pl.BlockSpec(shape, lambda b, c: (0,) * len(shape))

    return pl.pallas_call(
        _retention_kernel,
        grid=(B, nb),
        in_specs=[
            pl.BlockSpec((RET_RB, 1), lambda b, c: (b * nb + c, 0)),
            full((1, half)),
            col(0), col(1), col(2), col(3),
            full((RET_HEADS, L, L)), full((RET_HEADS, L, 1)), full((RET_HEADS, L, 1)),
            full((RET_HEADS, 1, 1)),
        ],
        out_specs=pl.BlockSpec((RET_RB, RET_WIDTH), lambda b, c: (b * nb + c, 0)),
        out_shape=jax.ShapeDtypeStruct((T, RET_WIDTH), BF16),
        scratch_shapes=[pltpu.VMEM((RET_HEADS, RET_HEAD_DIM, RET_HEAD_DIM), F32)],
        compiler_params=_cparams(("parallel", "arbitrary")),
        name="retention",
    )(pos2, inv_freq, proj, proj, proj, proj, d_intra, d_q, d_k, d_chunk)


def _attention_kernel(q_ref, k_ref, v_ref, tab_ref, o_ref):
    qb = pl.program_id(1)
    scale = ATT_HEAD_DIM ** -0.5

    def run(k_start, n_keys, col0):
        for h in range(ATT_HEADS):
            c0 = h * ATT_HEAD_DIM
            q = q_ref[:, c0:c0 + ATT_HEAD_DIM]
            k = k_ref[pl.ds(k_start, n_keys), c0:c0 + ATT_HEAD_DIM]
            v = v_ref[pl.ds(k_start, n_keys), c0:c0 + ATT_HEAD_DIM]
            s = lax.dot_general(q, k, (((1,), (1,)), ((), ())), preferred_element_type=F32)
            s = s * scale + tab_ref[h, :, col0:col0 + n_keys]
            m = jnp.max(s, axis=-1, keepdims=True)
            e = jnp.exp(s - m)
            p = e / jnp.sum(e, axis=-1, keepdims=True)
            o = jnp.dot(p.astype(BF16), v, preferred_element_type=F32)
            o_ref[:, c0:c0 + ATT_HEAD_DIM] = o.astype(o_ref.dtype)

    @pl.when(qb == 0)
    def _():
        run(0, ATT_QBLK, ATT_LEFT)

    @pl.when(qb == 1)
    def _():
        run(0, 2 * ATT_QBLK, ATT_QBLK)

    @pl.when(qb >= 2)
    def _():
        run(pl.multiple_of((qb - 2) * ATT_QBLK, ATT_QBLK), ATT_WIN, 0)


def _attention(proj, table, B, S):
    T = B * S
    nq = S // ATT_QBLK
    return pl.pallas_call(
        _attention_kernel,
        grid=(B, nq),
        in_specs=[
            pl.BlockSpec((ATT_QBLK, ATT_WIDTH), lambda b, q: (b * nq + q, 4)),
            pl.BlockSpec((S, ATT_WIDTH), lambda b, q: (b, 5)),
            pl.BlockSpec((S, ATT_WIDTH), lambda b, q: (b, 6)),
            pl.BlockSpec((ATT_HEADS, ATT_QBLK, ATT_WIN), lambda b, q: (0, 0, 0)),
        ],
        out_specs=pl.BlockSpec((ATT_QBLK, ATT_WIDTH), lambda b, q: (b * nq + q, 0)),
        out_shape=jax.ShapeDtypeStruct((T, ATT_WIDTH), BF16),
        compiler_params=_cparams(("parallel", "arbitrary")),
        name="attention",
    )(proj, proj, proj, table)


OR_TM = 256


def _out_router_kernel(x_ref, ret_ref, att_ref, wo_ref, g_ref, rw_ref, rb_ref,
                       h_ref, hn_ref, ti_ref, gt_ref, rk_ref, cnt_ref, carry_ref):
    @pl.when(pl.program_id(0) == 0)
    def _():
        carry_ref[...] = jnp.zeros_like(carry_ref)

    mix = (jnp.dot(ret_ref[...], wo_ref[:RET_WIDTH, :], preferred_element_type=F32)
           + jnp.dot(att_ref[...], wo_ref[RET_WIDTH:, :], preferred_element_type=F32))
    h = x_ref[...] + mix
    h_ref[...] = h
    hn = h * lax.rsqrt(jnp.mean(h * h, axis=-1, keepdims=True) + NORM_EPS) * g_ref[...]
    hn_bf = hn.astype(BF16)
    for c in range(ROW_C):
        hn_ref[pl.ds(c, OR_TM, stride=ROW_C), :] = hn[:, c * LANES:(c + 1) * LANES]

    hn_lo = (hn - hn_bf.astype(F32)).astype(BF16)
    rw = rw_ref[...]
    rw_hi = rw.astype(BF16)
    rw_lo = (rw - rw_hi.astype(F32)).astype(BF16)
    logits = (jnp.dot(hn_bf, rw_hi, preferred_element_type=F32)
              + jnp.dot(hn_bf, rw_lo, preferred_element_type=F32)
              + jnp.dot(hn_lo, rw_hi, preferred_element_type=F32)) + rb_ref[...]

    lane = lax.broadcasted_iota(jnp.int32, (OR_TM, N_EXPERTS), 1)
    work = logits
    vals, idxs = [], []
    for _ in range(TOP_K):
        m = jnp.max(work, axis=-1, keepdims=True)
        idx = jnp.min(jnp.where(work == m, lane, N_EXPERTS), axis=-1, keepdims=True)
        vals.append(m)
        idxs.append(idx)
        work = jnp.where(lane == idx, -jnp.inf, work)
    es = [jnp.exp(v - vals[0]) for v in vals]
    den = es[0] + es[1] + es[2] + es[3]

    sel = jnp.zeros((OR_TM, N_EXPERTS), F32)
    for idx in idxs:
        sel = sel + (lane == idx).astype(F32)
    r = lax.broadcasted_iota(jnp.int32, (OR_TM, OR_TM), 0)
    cc = lax.broadcasted_iota(jnp.int32, (OR_TM, OR_TM), 1)
    tri = (cc < r).astype(BF16)
    cum = jnp.dot(tri, sel.astype(BF16), preferred_element_type=F32) + carry_ref[...]
    for kk in range(TOP_K):
        rank = jnp.sum(jnp.where(lane == idxs[kk], cum, 0.0), axis=-1, keepdims=True)
        rk_ref[:, kk:kk + 1] = rank.astype(jnp.int32)
        ti_ref[:, kk:kk + 1] = idxs[kk]
        gt_ref[:, kk:kk + 1] = es[kk] / den
    carry_ref[...] = carry_ref[...] + jnp.sum(sel, axis=0, keepdims=True)
    cnt_ref[...] = carry_ref[...].astype(jnp.int32)


def _out_router(x2, ret, att, wo_bf, g, rw, rb):
    T = x2.shape[0]
    tm = OR_TM
    row = lambda n: pl.BlockSpec((tm, n), lambda i: (i, 0))
    fix = lambda a, b: pl.BlockSpec((a, b), lambda i: (0, 0))
    return pl.pallas_call(
        _out_router_kernel,
        grid=(T // tm,),
        in_specs=[row(D_MODEL), row(RET_WIDTH), row(ATT_WIDTH), fix(D_MODEL, D_MODEL),
                  fix(1, D_MODEL), fix(D_MODEL, N_EXPERTS), fix(1, N_EXPERTS)],
        out_specs=[row(D_MODEL), pl.BlockSpec((tm * ROW_C, LANES), lambda i: (i, 0)),
                   row(TOP_K), row(TOP_K), row(TOP_K), fix(1, N_EXPERTS)],
        out_shape=[
            jax.ShapeDtypeStruct((T, D_MODEL), F32),
            jax.ShapeDtypeStruct((T * ROW_C, LANES), F32),
            jax.ShapeDtypeStruct((T, TOP_K), jnp.int32),
            jax.ShapeDtypeStruct((T, TOP_K), F32),
            jax.ShapeDtypeStruct((T, TOP_K), jnp.int32),
            jax.ShapeDtypeStruct((1, N_EXPERTS), jnp.int32),
        ],
        scratch_shapes=[pltpu.VMEM((1, N_EXPERTS), F32)],
        compiler_params=_cparams(("arbitrary",)),
        name="out_router",
    )(x2, ret, att, wo_bf, g, rw, rb)


def _experts_kernel(ge_ref, gblk_ref, grows_ref, tok_ref, toknext_ref, yrow_ref, hn_hbm,
                    wg_ref, bg_ref, wu_ref, bu_ref, wd_ref, bd_ref, y_hbm,
                    xg, xbf, acc, stage, sem_g, sem_s):
    g = pl.program_id(0)
    f = pl.program_id(1)
    n_f = pl.num_programs(1)
    n_g = pl.num_programs(0)
    nrows = grows_ref[g]
    nsub = lax.shift_right_logical(nrows + (SUB_ROWS - 1), SUB_SHIFT)
    used = nrows > 0
    prev_used = (g > 0) & (grows_ref[jnp.maximum(g - 1, 0)] > 0)
    n_sub_max = ROW_GROUP // SUB_ROWS
    blk_rows = SUB_ROWS * ROW_C

    def gather_copy(idx_ref, r):
        src0 = pl.multiple_of(idx_ref[0, 0, r] * ROW_C, ROW_C)
        dst0 = pl.multiple_of(r * ROW_C, ROW_C)
        sem = sem_g.at[lax.shift_right_logical(r, SUB_SHIFT)]
        return pltpu.make_async_copy(hn_hbm.at[pl.ds(src0, ROW_C)], xg.at[pl.ds(dst0, ROW_C)], sem)

    def gather_wait_all():
        for s in range(n_sub_max):
            blk = xg.at[pl.ds(s * blk_rows, blk_rows)]
            pltpu.make_async_copy(blk, blk, sem_g.at[s]).wait()

    def scatter_copy(s, r0, j):
        src0 = pl.multiple_of(j * ROW_C, ROW_C)
        dst0 = pl.multiple_of(yrow_ref[0, 0, r0 + j] * ROW_C, ROW_C)
        return pltpu.make_async_copy(stage.at[s, pl.ds(src0, ROW_C)], y_hbm.at[pl.ds(dst0, ROW_C)], sem_s)

    def scatter_wait_block(s):
        blk = stage.at[s]
        pltpu.make_async_copy(blk, blk, sem_s).wait()

    def stage_rows(s, val):
        for c in range(ROW_C):
            stage[s, pl.ds(c, SUB_ROWS, stride=ROW_C), :] = val[:, c * LANES:(c + 1) * LANES]

    def scatter_partial(s, r0):
        n_here = jnp.minimum(nrows - r0, SUB_ROWS)

        def start(j, _):
            scatter_copy(s, r0, j).start()
            return 0

        def wait(j, _):
            scatter_copy(s, r0, j).wait()
            return 0

        lax.fori_loop(0, n_here, start, 0)
        lax.fori_loop(0, n_here, wait, 0)

    @pl.when(used & (f == 0) & (g == 0))
    def _():
        def start(r, _):
            gather_copy(tok_ref, r).start()
            return 0

        lax.fori_loop(0, ROW_GROUP, start, 0)

    @pl.when((f == 0) & (used | prev_used))
    def _():
        gather_wait_all()

    @pl.when(used & (f == 0))
    def _():
        def unpack(s, _):
            r0 = pl.multiple_of(s * SUB_ROWS, SUB_ROWS)
            for c in range(ROW_C):
                v = xg[pl.ds(r0 * ROW_C + c, SUB_ROWS, stride=ROW_C), :]
                xbf[pl.ds(r0, SUB_ROWS), c * LANES:(c + 1) * LANES] = v.astype(BF16)
            acc[pl.ds(r0, SUB_ROWS), :] = jnp.broadcast_to(bd_ref[0], (SUB_ROWS, D_MODEL))
            return 0

        lax.fori_loop(0, jnp.where(nsub > 1, n_sub_max, 1), unpack, 0)

    def prefetch_next():
        for j in range(GATHER_ROWS):
            gather_copy(toknext_ref, f * GATHER_ROWS + j).start()

    def hidden(m):
        xb = xbf[0:m, :]
        gate = jnp.dot(xb, wg_ref[0].astype(BF16), preferred_element_type=F32) + bg_ref[0]
        gate = jnp.minimum(gate, SWIGLU_LIMIT)
        up = jnp.dot(xb, wu_ref[0].astype(BF16), preferred_element_type=F32) + bu_ref[0]
        up = jnp.clip(up, -SWIGLU_LIMIT, SWIGLU_LIMIT)
        return ((up + 1.0) * gate * jax.nn.sigmoid(SWIGLU_ALPHA * gate)).astype(BF16)

    def mlp_step(m):
        prefetch_next()
        acc[0:m, :] += jnp.dot(hidden(m), wd_ref[0].astype(BF16), preferred_element_type=F32)

    def mlp_last_full():
        prefetch_next()
        hid = hidden(ROW_GROUP)
        wd = wd_ref[0].astype(BF16)
        for s in range(n_sub_max):
            r0 = s * SUB_ROWS
            out = acc[r0:r0 + SUB_ROWS, :] + jnp.dot(hid[r0:r0 + SUB_ROWS, :], wd, preferred_element_type=F32)
            stage_rows(s, out)
            if s < n_sub_max - 1:
                for j in range(SUB_ROWS):
                    scatter_copy(s, r0, j).start()

    is_last = f == n_f - 1
    full = nsub == n_sub_max
    pl.when((nsub == 1))(functools.partial(mlp_step, SUB_ROWS))
    pl.when((nsub > 1) & jnp.logical_not(is_last & full))(functools.partial(mlp_step, ROW_GROUP))
    pl.when(is_last & full)(mlp_last_full)

    @pl.when(is_last & full)
    def _():
        scatter_partial(n_sub_max - 1, (n_sub_max - 1) * SUB_ROWS)
        for s in range(n_sub_max - 1):
            scatter_wait_block(s)

    @pl.when(is_last & used & jnp.logical_not(full))
    def _():
        def emit(s, _):
            r0 = pl.multiple_of(s * SUB_ROWS, SUB_ROWS)
            stage_rows(0, acc[pl.ds(r0, SUB_ROWS), :])
            scatter_partial(0, r0)
            return 0

        lax.fori_loop(0, nsub, emit, 0)

    @pl.when(used & is_last & (g == n_g - 1))
    def _():
        gather_wait_all()


def _experts(g_e, g_blk, g_rows, tok, yrow, hn2p, w_gate, b_gate, w_up, b_up, w_down, b_down, n_groups, n_yrows):
    nf = D_FF // FF_TILE
    bg3 = b_gate.reshape(N_EXPERTS, 1, D_FF)
    bu3 = b_up.reshape(N_EXPERTS, 1, D_FF)
    bd3 = b_down.reshape(N_EXPERTS, 1, D_MODEL)
    tok3 = tok.reshape(n_groups, 1, ROW_GROUP)
    yrow3 = yrow.reshape(n_groups, 1, ROW_GROUP)

    def ff(gi, fi, ge, gb, gn):
        return jnp.where(gn[gi] > 0, fi, nf - 1)

    grid_spec = pltpu.PrefetchScalarGridSpec(
        num_scalar_prefetch=3,
        grid=(n_groups, nf),
        in_specs=[
            pl.BlockSpec((1, 1, ROW_GROUP), lambda gi, fi, ge, gb, gn: (gb[gi], 0, 0), memory_space=pltpu.SMEM),
            pl.BlockSpec((1, 1, ROW_GROUP), lambda gi, fi, ge, gb, gn: (gb[jnp.minimum(gi + 1, n_groups - 1)], 0, 0),
                         memory_space=pltpu.SMEM),
            pl.BlockSpec((1, 1, ROW_GROUP), lambda gi, fi, ge, gb, gn: (gb[gi], 0, 0), memory_space=pltpu.SMEM),
            pl.BlockSpec(memory_space=pl.ANY),
            pl.BlockSpec((1, D_MODEL, FF_TILE), lambda gi, fi, ge, gb, gn: (ge[gi], 0, ff(gi, fi, ge, gb, gn))),
            pl.BlockSpec((1, 1, FF_TILE), lambda gi, fi, ge, gb, gn: (ge[gi], 0, ff(gi, fi, ge, gb, gn))),
            pl.BlockSpec((1, D_MODEL, FF_TILE), lambda gi, fi, ge, gb, gn: (ge[gi], 0, ff(gi, fi, ge, gb, gn))),
            pl.BlockSpec((1, 1, FF_TILE), lambda gi, fi, ge, gb, gn: (ge[gi], 0, ff(gi, fi, ge, gb, gn))),
            pl.BlockSpec((1, FF_TILE, D_MODEL), lambda gi, fi, ge, gb, gn: (ge[gi], ff(gi, fi, ge, gb, gn), 0)),
            pl.BlockSpec((1, 1, D_MODEL), lambda gi, fi, ge, gb, gn: (ge[gi], 0, 0)),
        ],
        out_specs=pl.BlockSpec(memory_space=pl.ANY),
        scratch_shapes=[
            pltpu.VMEM((ROW_GROUP * ROW_C, LANES), F32),
            pltpu.VMEM((ROW_GROUP, D_MODEL), BF16),
            pltpu.VMEM((ROW_GROUP, D_MODEL), F32),
            pltpu.VMEM((ROW_GROUP // SUB_ROWS, SUB_ROWS * ROW_C, LANES), F32),
            pltpu.SemaphoreType.DMA((ROW_GROUP // SUB_ROWS,)), pltpu.SemaphoreType.DMA(()),
        ],
    )
    return pl.pallas_call(
        _experts_kernel,
        grid_spec=grid_spec,
        out_shape=jax.ShapeDtypeStruct((n_yrows * ROW_C, LANES), F32),
        compiler_params=_cparams(("arbitrary", "arbitrary")),
        name="experts",
    )(g_e, g_blk, g_rows, tok3, tok3, yrow3, hn2p, w_gate, bg3, w_up, bu3, w_down, bd3)


CB_TM = 256


def _combine_kernel(h_ref, gt_ref, g_ref, y0_ref, y1_ref, y2_ref, y3_ref, o_ref, h2_ref):
    gt = gt_ref[...]
    ys = (y0_ref, y1_ref, y2_ref, y3_ref)
    for c in range(ROW_C):
        y = ys[0][pl.ds(c, CB_TM, stride=ROW_C), :] * gt[:, 0:1]
        for kk in range(1, TOP_K):
            y = y + ys[kk][pl.ds(c, CB_TM, stride=ROW_C), :] * gt[:, kk:kk + 1]
        h2_ref[:, c * LANES:(c + 1) * LANES] = h_ref[:, c * LANES:(c + 1) * LANES] + y
    h = h2_ref[...]
    o_ref[...] = h * lax.rsqrt(jnp.mean(h * h, axis=-1, keepdims=True) + NORM_EPS) * g_ref[...]


def _combine(h, gates, g_final, ybuf):
    T = h.shape[0]
    steps = T // CB_TM

    def yspec(kk):
        return pl.BlockSpec((CB_TM * ROW_C, LANES), lambda i: (kk * steps + i, 0))

    return pl.pallas_call(
        _combine_kernel,
        grid=(steps,),
        in_specs=[
            pl.BlockSpec((CB_TM, D_MODEL), lambda i: (i, 0)),
            pl.BlockSpec((CB_TM, TOP_K), lambda i: (i, 0)),
            pl.BlockSpec((1, D_MODEL), lambda i: (0, 0)),
            yspec(0), yspec(1), yspec(2), yspec(3),
        ],
        out_specs=pl.BlockSpec((CB_TM, D_MODEL), lambda i: (i, 0)),
        out_shape=jax.ShapeDtypeStruct((T, D_MODEL), F32),
        scratch_shapes=[pltpu.VMEM((CB_TM, D_MODEL), F32)],
        compiler_params=_cparams(("parallel",)),
        name="combine",
    )(h, gates, g_final, ybuf, ybuf, ybuf, ybuf)


def _group_tables(counts, top_i, rank, n_groups, T):
    A = T * TOP_K
    P = n_groups * ROW_GROUP
    counts = counts.reshape(N_EXPERTS)
    groups_e = (counts + ROW_GROUP - 1) // ROW_GROUP
    gend = jnp.cumsum(groups_e)
    gstart = gend - groups_e
    total = gend[-1]
    dest = ((gstart * ROW_GROUP)[top_i] + rank).astype(jnp.int32)
    gidx = jnp.arange(n_groups, dtype=jnp.int32)
    used = gidx < total
    last = jnp.maximum(total - 1, 0)
    gsafe = jnp.where(used, gidx, last)
    g_e = jnp.minimum(jnp.sum(gend[None, :] <= gsafe[:, None], axis=1), N_EXPERTS - 1).astype(jnp.int32)
    rows = jnp.clip(counts[g_e] - (gsafe - gstart[g_e]) * ROW_GROUP, 0, ROW_GROUP)
    g_rows = jnp.where(used, rows, 0).astype(jnp.int32)
    src = jnp.zeros((P,), jnp.int32).at[dest.reshape(A)].set(jnp.arange(A, dtype=jnp.int32))
    tok = src // TOP_K
    yrow = (src % TOP_K) * T + tok
    return g_e, gsafe.astype(jnp.int32), g_rows, tok, yrow


def kernel(x, positions, norm_mix_g, w_in, w_out, rel_bias, norm_ffn_g, router_w, router_b,
           w_gate, b_gate, w_up, b_up, w_down, b_down, norm_final_g):
    B, S, D = x.shape
    T = B * S
    depth = norm_mix_g.shape[0]
    assert depth == 1, "the combine kernel applies the final norm, so exactly one layer is supported"
    n_groups = (T * TOP_K) // ROW_GROUP + N_EXPERTS
    h = x.reshape(T, D)
    pos2 = positions.reshape(T, 1)
    for l in range(depth):
        proj = _in_proj(h, norm_mix_g[l].reshape(1, D), w_in[l].astype(BF16))
        table = _bias_table(rel_bias[l])
        ret = _retention(proj, pos2, B, S)
        att = _attention(proj, table, B, S)
        h, hn2, top_i, gates, rank, counts = _out_router(
            h, ret, att, w_out[l].astype(BF16), norm_ffn_g[l].reshape(1, D),
            router_w[l], router_b[l].reshape(1, N_EXPERTS))
        g_e, g_blk, g_rows, tok, yrow = _group_tables(counts, top_i, rank, n_groups, T)
        ybuf = _experts(g_e, g_blk, g_rows, tok, yrow, hn2, w_gate[l], b_gate[l], w_up[l], b_up[l],
                        w_down[l], b_down[l], n_groups, T * TOP_K)
        h = _combine(h, gates, norm_final_g.reshape(1, D), ybuf)
    return h.reshape(B, S, D)
```

```python
import functools
import math

import jax
import jax.numpy as jnp
import numpy as np
from jax import lax
from jax.experimental import pallas as pl
from jax.experimental.pallas import tpu as pltpu

D_MODEL = 2048
CHUNK = 64
RET_WIDTH = 1024
ATT_WIDTH = 1024
RET_HEAD_DIM = 256
RET_HEADS = RET_WIDTH // RET_HEAD_DIM
ATT_HEAD_DIM = 128
ATT_HEADS = ATT_WIDTH // ATT_HEAD_DIM
LEFT_CHUNKS = 8
MAX_REL = 256
REL_SIZE = MAX_REL + CHUNK
ROPE_BASE = 10000.0
N_EXPERTS = 32
TOP_K = 4
D_FF = D_MODEL
SWIGLU_LIMIT = 7.0
SWIGLU_ALPHA = 1.702
NORM_EPS = 1e-6
IN_COLS = 4 * RET_WIDTH + 3 * ATT_WIDTH
NEG_INF = -1e30

BF16 = jnp.bfloat16
F32 = jnp.float32

VMEM_LIMIT_BYTES = 56 * 1024 * 1024

ATT_QBLK = 256
ATT_LEFT = LEFT_CHUNKS * CHUNK
ATT_WIN = ATT_LEFT + ATT_QBLK
ROLL_W = 1024

ROW_GROUP = 1280
SUB_ROWS = 256
SUB_SHIFT = SUB_ROWS.bit_length() - 1
assert 1 << SUB_SHIFT == SUB_ROWS
FF_TILE = 256
LANES = 128
ROW_C = D_MODEL // LANES
GATHER_ROWS = ROW_GROUP // (D_FF // FF_TILE)
BIG_SUBS = (ROW_GROUP // SUB_ROWS - 1, ROW_GROUP // SUB_ROWS)


def _cparams(sem):
    return pltpu.CompilerParams(dimension_semantics=sem, vmem_limit_bytes=VMEM_LIMIT_BYTES)


def _in_proj_kernel(x_ref, g_ref, w_ref, o_ref, hn_ref):
    @pl.when(pl.program_id(1) == 0)
    def _():
        x = x_ref[...]
        ms = jnp.mean(x * x, axis=-1, keepdims=True)
        hn_ref[...] = (x * lax.rsqrt(ms + NORM_EPS) * g_ref[...]).astype(BF16)

    o_ref[...] = jnp.dot(hn_ref[...], w_ref[...], preferred_element_type=F32).astype(o_ref.dtype)


def _in_proj(x2, g, w_bf):
    T = x2.shape[0]
    tm, tn = 1024, 1024
    return pl.pallas_call(
        _in_proj_kernel,
        grid=(T // tm, IN_COLS // tn),
        in_specs=[
            pl.BlockSpec((tm, D_MODEL), lambda i, j: (i, 0)),
            pl.BlockSpec((1, D_MODEL), lambda i, j: (0, 0)),
            pl.BlockSpec((D_MODEL, tn), lambda i, j: (0, j)),
        ],
        out_specs=pl.BlockSpec((tm, tn), lambda i, j: (i, j)),
        out_shape=jax.ShapeDtypeStruct((T, IN_COLS), BF16),
        scratch_shapes=[pltpu.VMEM((tm, D_MODEL), BF16)],
        compiler_params=_cparams(("parallel", "arbitrary")),
        name="in_proj",
    )(x2, g, w_bf)


def _bias_table_kernel(sel_ref, rb_ref, o_ref):
    rb = rb_ref[...]
    hi = rb.astype(BF16)
    r1 = rb - hi.astype(F32)
    mid = r1.astype(BF16)
    lo = (r1 - mid.astype(F32)).astype(BF16)
    sel = sel_ref[...]
    frow = (jnp.dot(hi, sel, preferred_element_type=F32)
            + jnp.dot(mid, sel, preferred_element_type=F32)
            + jnp.dot(lo, sel, preferred_element_type=F32))
    i = lax.broadcasted_iota(jnp.int32, (ATT_QBLK, ATT_WIN), 0)
    c = lax.broadcasted_iota(jnp.int32, (ATT_QBLK, ATT_WIN), 1)
    qc = (i + ATT_LEFT) // CHUNK
    kc = c // CHUNK
    valid = (kc <= qc) & (kc >= qc - LEFT_CHUNKS)
    for h in range(ATT_HEADS):
        rows = jnp.broadcast_to(frow[h:h + 1, :], (ATT_QBLK, ROLL_W))
        rolled = pltpu.roll(rows, 0, 1, stride=1, stride_axis=0)
        o_ref[h] = jnp.where(valid, rolled[:, :ATT_WIN], NEG_INF)


def _bias_table(rel_bias):
    u = np.arange(-(ATT_QBLK - 1), ATT_WIN)
    idx = np.clip(ATT_LEFT - u, -(CHUNK - 1), MAX_REL) + (CHUNK - 1)
    sel = np.zeros((REL_SIZE, ROLL_W), np.float32)
    sel[idx, u % ROLL_W] = 1.0
    sel = jnp.asarray(sel, BF16)
    return pl.pallas_call(
        _bias_table_kernel,
        out_shape=jax.ShapeDtypeStruct((ATT_HEADS, ATT_QBLK, ATT_WIN), F32),
        compiler_params=pltpu.CompilerParams(vmem_limit_bytes=VMEM_LIMIT_BYTES),
        name="bias_table",
    )(sel, rel_bias)


RET_RB = 256


def _retention_kernel(pos_ref, invf_ref, q_ref, k_ref, v_ref, g_ref,
                      dintra_ref, dq_ref, dk_ref, dchunk_ref, o_ref, state_ref):
    @pl.when(pl.program_id(1) == 0)
    def _():
        state_ref[...] = jnp.zeros_like(state_ref)

    half = RET_HEAD_DIM // 2
    ang = pos_ref[...].astype(F32) * invf_ref[...]
    cos = jnp.cos(ang)
    sin = jnp.sin(ang)
    kscale = RET_HEAD_DIM ** -0.5

    for h in range(RET_HEADS):
        c0 = h * RET_HEAD_DIM
        q = q_ref[:, c0:c0 + RET_HEAD_DIM].astype(F32)
        k = k_ref[:, c0:c0 + RET_HEAD_DIM].astype(F32)
        q1, q2 = q[:, :half], q[:, half:]
        k1, k2 = k[:, :half], k[:, half:]
        qr = jnp.concatenate([q1 * cos - q2 * sin, q1 * sin + q2 * cos], axis=-1)
        kr = jnp.concatenate([k1 * cos - k2 * sin, k1 * sin + k2 * cos], axis=-1) * kscale
        d_intra = dintra_ref[h]
        d_q = dq_ref[h]
        d_k = dk_ref[h]
        d_c = dchunk_ref[h]
        for c in range(RET_RB // CHUNK):
            r0 = c * CHUNK
            qc = qr[r0:r0 + CHUNK]
            kc = kr[r0:r0 + CHUNK]
            vc = v_ref[r0:r0 + CHUNK, c0:c0 + RET_HEAD_DIM]
            qb = qc.astype(BF16)
            s = lax.dot_general(qb, kc.astype(BF16), (((1,), (1,)), ((), ())),
                                preferred_element_type=F32) * d_intra
            o = jnp.dot(s.astype(BF16), vc, preferred_element_type=F32)
            st = state_ref[h]
            o = o + jnp.dot((qc * d_q).astype(BF16), st.astype(BF16), preferred_element_type=F32)
            kv = lax.dot_general((kc * d_k).astype(BF16), vc, (((0,), (0,)), ((), ())),
                                 preferred_element_type=F32)
            state_ref[h] = st * d_c + kv
            o = o * lax.rsqrt(jnp.mean(o * o, axis=-1, keepdims=True) + NORM_EPS)
            g = g_ref[r0:r0 + CHUNK, c0:c0 + RET_HEAD_DIM].astype(F32)
            o_ref[r0:r0 + CHUNK, c0:c0 + RET_HEAD_DIM] = (o * (g * jax.nn.sigmoid(g))).astype(o_ref.dtype)


def _retention(proj, pos2, B, S):
    T = B * S
    nb = S // RET_RB
    half = RET_HEAD_DIM // 2
    L = CHUNK
    inv_freq = (1.0 / (ROPE_BASE ** (jnp.arange(half, dtype=F32) / half))).reshape(1, half)
    log_gamma = jnp.log(1.0 - jnp.exp(jnp.linspace(math.log(1.0 / 32), math.log(1.0 / 512), RET_HEADS)))
    n = jnp.arange(L, dtype=F32)
    d_intra = jnp.exp(jnp.abs(n[:, None] - n[None, :])[None] * log_gamma[:, None, None]).astype(F32)
    d_q = jnp.exp((n[None, :] + 1.0) * log_gamma[:, None]).astype(F32)[:, :, None]
    d_k = jnp.exp((L - 1.0 - n)[None, :] * log_gamma[:, None]).astype(F32)[:, :, None]
    d_chunk = jnp.exp(L * log_gamma).astype(F32).reshape(RET_HEADS, 1, 1)

    def col(j):
        return pl.BlockSpec((RET_RB, RET_WIDTH), lambda b, c: (b * nb + c, j))

    def full(shape):
        return pl.BlockSpec(shape, lambda b, c: (0,) * len(shape))

    return pl.pallas_call(
        _retention_kernel,
        grid=(B, nb),
        in_specs=[
            pl.BlockSpec((RET_RB, 1), lambda b, c: (b * nb + c, 0)),
            full((1, half)),
            col(0), col(1), col(2), col(3),
            full((RET_HEADS, L, L)), full((RET_HEADS, L, 1)), full((RET_HEADS, L, 1)),
            full((RET_HEADS, 1, 1)),
        ],
        out_specs=pl.BlockSpec((RET_RB, RET_WIDTH), lambda b, c: (b * nb + c, 0)),
        out_shape=jax.ShapeDtypeStruct((T, RET_WIDTH), BF16),
        scratch_shapes=[pltpu.VMEM((RET_HEADS, RET_HEAD_DIM, RET_HEAD_DIM), F32)],
        compiler_params=_cparams(("parallel", "arbitrary")),
        name="retention",
    )(pos2, inv_freq, proj, proj, proj, proj, d_intra, d_q, d_k, d_chunk)


def _attention_kernel(q_ref, k_ref, v_ref, tab_ref, o_ref):
    qb = pl.program_id(1)
    scale = ATT_HEAD_DIM ** -0.5

    def run(k_start, n_keys, col0):
        for h in range(ATT_HEADS):
            c0 = h * ATT_HEAD_DIM
            q = q_ref[:, c0:c0 + ATT_HEAD_DIM]
            k = k_ref[pl.ds(k_start, n_keys), c0:c0 + ATT_HEAD_DIM]
            v = v_ref[pl.ds(k_start, n_keys), c0:c0 + ATT_HEAD_DIM]
            s = lax.dot_general(q, k, (((1,), (1,)), ((), ())), preferred_element_type=F32)
            s = s * scale + tab_ref[h, :, col0:col0 + n_keys]
            m = jnp.max(s, axis=-1, keepdims=True)
            e = jnp.exp(s - m)
            o = jnp.dot(e.astype(BF16), v, preferred_element_type=F32) / jnp.sum(e, axis=-1, keepdims=True)
            o_ref[:, c0:c0 + ATT_HEAD_DIM] = o.astype(o_ref.dtype)

    @pl.when(qb == 0)
    def _():
        run(0, ATT_QBLK, ATT_LEFT)

    @pl.when(qb == 1)
    def _():
        run(0, 2 * ATT_QBLK, ATT_QBLK)

    @pl.when(qb >= 2)
    def _():
        run(pl.multiple_of((qb - 2) * ATT_QBLK, ATT_QBLK), ATT_WIN, 0)


def _attention(proj, table, B, S):
    T = B * S
    nq = S // ATT_QBLK
    return pl.pallas_call(
        _attention_kernel,
        grid=(B, nq),
        in_specs=[
            pl.BlockSpec((ATT_QBLK, ATT_WIDTH), lambda b, q: (b * nq + q, 4)),
            pl.BlockSpec((S, ATT_WIDTH), lambda b, q: (b, 5)),
            pl.BlockSpec((S, ATT_WIDTH), lambda b, q: (b, 6)),
            pl.BlockSpec((ATT_HEADS, ATT_QBLK, ATT_WIN), lambda b, q: (0, 0, 0)),
        ],
        out_specs=pl.BlockSpec((ATT_QBLK, ATT_WIDTH), lambda b, q: (b * nq + q, 0)),
        out_shape=jax.ShapeDtypeStruct((T, ATT_WIDTH), BF16),
        compiler_params=_cparams(("parallel", "arbitrary")),
        name="attention",
    )(proj, proj, proj, table)


OR_TM = 512


def _out_router_kernel(x_ref, ret_ref, att_ref, wo_ref, g_ref, rw_ref, rb_ref,
                       h_ref, hn_ref, ti_ref, gt_ref, rk_ref, cnt_ref, carry_ref):
    @pl.when(pl.program_id(0) == 0)
    def _():
        carry_ref[...] = jnp.zeros_like(carry_ref)

    mix = (jnp.dot(ret_ref[...], wo_ref[:RET_WIDTH, :], preferred_element_type=F32)
           + jnp.dot(att_ref[...], wo_ref[RET_WIDTH:, :], preferred_element_type=F32))
    h = x_ref[...] + mix
    h_ref[...] = h
    hn = h * lax.rsqrt(jnp.mean(h * h, axis=-1, keepdims=True) + NORM_EPS) * g_ref[...]
    hn_bf = hn.astype(BF16)
    for c in range(ROW_C):
        hn_ref[pl.ds(c, OR_TM, stride=ROW_C), :] = hn[:, c * LANES:(c + 1) * LANES]

    hn_lo = (hn - hn_bf.astype(F32)).astype(BF16)
    rw = rw_ref[...]
    rw_hi = rw.astype(BF16)
    rw_lo = (rw - rw_hi.astype(F32)).astype(BF16)
    logits = (jnp.dot(hn_bf, rw_hi, preferred_element_type=F32)
              + jnp.dot(hn_bf, rw_lo, preferred_element_type=F32)
              + jnp.dot(hn_lo, rw_hi, preferred_element_type=F32)) + rb_ref[...]

    lane = lax.broadcasted_iota(jnp.int32, (OR_TM, N_EXPERTS), 1)
    work = logits
    vals, idxs = [], []
    for _ in range(TOP_K):
        m = jnp.max(work, axis=-1, keepdims=True)
        idx = jnp.min(jnp.where(work == m, lane, N_EXPERTS), axis=-1, keepdims=True)
        vals.append(m)
        idxs.append(idx)
        work = jnp.where(lane == idx, -jnp.inf, work)
    es = [jnp.exp(v - vals[0]) for v in vals]
    den = es[0] + es[1] + es[2] + es[3]

    sel = jnp.zeros((OR_TM, N_EXPERTS), F32)
    for idx in idxs:
        sel = sel + (lane == idx).astype(F32)
    r = lax.broadcasted_iota(jnp.int32, (OR_TM, OR_TM), 0)
    cc = lax.broadcasted_iota(jnp.int32, (OR_TM, OR_TM), 1)
    tri = (cc < r).astype(BF16)
    cum = jnp.dot(tri, sel.astype(BF16), preferred_element_type=F32) + carry_ref[...]
    for kk in range(TOP_K):
        rank = jnp.sum(jnp.where(lane == idxs[kk], cum, 0.0), axis=-1, keepdims=True)
        rk_ref[:, kk:kk + 1] = rank.astype(jnp.int32)
        ti_ref[:, kk:kk + 1] = idxs[kk]
        gt_ref[:, kk:kk + 1] = es[kk] / den
    carry_ref[...] = carry_ref[...] + jnp.sum(sel, axis=0, keepdims=True)
    cnt_ref[...] = carry_ref[...].astype(jnp.int32)


def _out_router(x2, ret, att, wo_bf, g, rw, rb):
    T = x2.shape[0]
    tm = OR_TM
    row = lambda n: pl.BlockSpec((tm, n), lambda i: (i, 0))
    fix = lambda a, b: pl.BlockSpec((a, b), lambda i: (0, 0))
    return pl.pallas_call(
        _out_router_kernel,
        grid=(T // tm,),
        in_specs=[row(D_MODEL), row(RET_WIDTH), row(ATT_WIDTH), fix(D_MODEL, D_MODEL),
                  fix(1, D_MODEL), fix(D_MODEL, N_EXPERTS), fix(1, N_EXPERTS)],
        out_specs=[row(D_MODEL), pl.BlockSpec((tm * ROW_C, LANES), lambda i: (i, 0)),
                   row(TOP_K), row(TOP_K), row(TOP_K), fix(1, N_EXPERTS)],
        out_shape=[
            jax.ShapeDtypeStruct((T, D_MODEL), F32),
            jax.ShapeDtypeStruct((T * ROW_C, LANES), F32),
            jax.ShapeDtypeStruct((T, TOP_K), jnp.int32),
            jax.ShapeDtypeStruct((T, TOP_K), F32),
            jax.ShapeDtypeStruct((T, TOP_K), jnp.int32),
            jax.ShapeDtypeStruct((1, N_EXPERTS), jnp.int32),
        ],
        scratch_shapes=[pltpu.VMEM((1, N_EXPERTS), F32)],
        compiler_params=_cparams(("arbitrary",)),
        name="out_router",
    )(x2, ret, att, wo_bf, g, rw, rb)


def _experts_kernel(ge_ref, gblk_ref, grows_ref, tok_ref, toknext_ref, yrow_ref, hn_hbm,
                    wg_ref, bg_ref, wu_ref, bu_ref, wd_ref, bd_ref, y_hbm,
                    xg, xbf, acc, stage, sem_g, sem_s):
    g = pl.program_id(0)
    f = pl.program_id(1)
    n_f = pl.num_programs(1)
    n_g = pl.num_programs(0)
    nrows = grows_ref[g]
    nsub = lax.shift_right_logical(nrows + (SUB_ROWS - 1), SUB_SHIFT)
    used = nrows > 0
    prev_used = (g > 0) & (grows_ref[jnp.maximum(g - 1, 0)] > 0)
    n_sub_max = ROW_GROUP // SUB_ROWS
    blk_rows = SUB_ROWS * ROW_C

    def gather_copy(idx_ref, r):
        src0 = pl.multiple_of(idx_ref[0, 0, r] * ROW_C, ROW_C)
        dst0 = pl.multiple_of(r * ROW_C, ROW_C)
        sem = sem_g.at[lax.shift_right_logical(r, SUB_SHIFT)]
        return pltpu.make_async_copy(hn_hbm.at[pl.ds(src0, ROW_C)], xg.at[pl.ds(dst0, ROW_C)], sem)

    def gather_wait_all():
        for s in range(n_sub_max):
            blk = xg.at[pl.ds(s * blk_rows, blk_rows)]
            pltpu.make_async_copy(blk, blk, sem_g.at[s]).wait()

    def scatter_copy(s, r0, j):
        src0 = pl.multiple_of(j * ROW_C, ROW_C)
        dst0 = pl.multiple_of(yrow_ref[0, 0, r0 + j] * ROW_C, ROW_C)
        return pltpu.make_async_copy(stage.at[s, pl.ds(src0, ROW_C)], y_hbm.at[pl.ds(dst0, ROW_C)], sem_s)

    def scatter_wait_block(s):
        blk = stage.at[s]
        pltpu.make_async_copy(blk, blk, sem_s).wait()

    def stage_rows(s, val):
        for c in range(ROW_C):
            stage[s, pl.ds(c, SUB_ROWS, stride=ROW_C), :] = val[:, c * LANES:(c + 1) * LANES]

    def scatter_partial(s, r0):
        n_here = jnp.minimum(nrows - r0, SUB_ROWS)

        def start(j, _):
            scatter_copy(s, r0, j).start()
            return 0

        def wait(j, _):
            scatter_copy(s, r0, j).wait()
            return 0

        lax.fori_loop(0, n_here, start, 0)
        lax.fori_loop(0, n_here, wait, 0)

    @pl.when(used & (f == 0) & (g == 0))
    def _():
        def start(r, _):
            gather_copy(tok_ref, r).start()
            return 0

        lax.fori_loop(0, ROW_GROUP, start, 0)

    @pl.when((f == 0) & (used | prev_used))
    def _():
        gather_wait_all()

    @pl.when(used & (f == 0))
    def _():
        def unpack(s, _):
            r0 = pl.multiple_of(s * SUB_ROWS, SUB_ROWS)
            for c in range(ROW_C):
                v = xg[pl.ds(r0 * ROW_C + c, SUB_ROWS, stride=ROW_C), :]
                xbf[pl.ds(r0, SUB_ROWS), c * LANES:(c + 1) * LANES] = v.astype(BF16)
            acc[pl.ds(r0, SUB_ROWS), :] = jnp.broadcast_to(bd_ref[0], (SUB_ROWS, D_MODEL))
            return 0

        n_read = jnp.where(nsub == 1, 1, jnp.where(nsub <= BIG_SUBS[0], BIG_SUBS[0], BIG_SUBS[1]))
        lax.fori_loop(0, n_read, unpack, 0)

    def prefetch_next():
        for j in range(GATHER_ROWS):
            gather_copy(toknext_ref, f * GATHER_ROWS + j).start()

    def hidden(m):
        xb = xbf[0:m, :]
        gate = jnp.dot(xb, wg_ref[0].astype(BF16), preferred_element_type=F32) + bg_ref[0]
        gate = jnp.minimum(gate, SWIGLU_LIMIT)
        up = jnp.dot(xb, wu_ref[0].astype(BF16), preferred_element_type=F32) + bu_ref[0]
        up = jnp.clip(up, -SWIGLU_LIMIT, SWIGLU_LIMIT)
        return ((up + 1.0) * gate * jax.nn.sigmoid(SWIGLU_ALPHA * gate)).astype(BF16)

    def mlp_step(m):
        prefetch_next()
        acc[0:m, :] += jnp.dot(hidden(m), wd_ref[0].astype(BF16), preferred_element_type=F32)

    def mlp_last_big(n):
        prefetch_next()
        hid = hidden(n * SUB_ROWS)
        wd = wd_ref[0].astype(BF16)
        for s in range(n):
            r0 = s * SUB_ROWS
            out = acc[r0:r0 + SUB_ROWS, :] + jnp.dot(hid[r0:r0 + SUB_ROWS, :], wd, preferred_element_type=F32)
            stage_rows(s, out)
            if s < n - 1:
                for j in range(SUB_ROWS):
                    scatter_copy(s, r0, j).start()

    is_last = f == n_f - 1
    mid, top = BIG_SUBS
    big_last = is_last & ((nsub == mid) | (nsub == top))
    pl.when(nsub == 1)(functools.partial(mlp_step, SUB_ROWS))
    pl.when((nsub > 1) & (nsub <= mid) & jnp.logical_not(big_last))(functools.partial(mlp_step, mid * SUB_ROWS))
    pl.when((nsub > mid) & jnp.logical_not(big_last))(functools.partial(mlp_step, top * SUB_ROWS))
    for n in BIG_SUBS:
        pl.when(is_last & (nsub == n))(functools.partial(mlp_last_big, n))

    @pl.when(big_last)
    def _():
        scatter_partial(nsub - 1, pl.multiple_of((nsub - 1) * SUB_ROWS, SUB_ROWS))

        def wait(s, _):
            scatter_wait_block(s)
            return 0

        lax.fori_loop(0, nsub - 1, wait, 0)

    @pl.when(is_last & used & jnp.logical_not(big_last))
    def _():
        def emit(s, _):
            r0 = pl.multiple_of(s * SUB_ROWS, SUB_ROWS)
            stage_rows(0, acc[pl.ds(r0, SUB_ROWS), :])
            scatter_partial(0, r0)
            return 0

        lax.fori_loop(0, nsub, emit, 0)

    @pl.when(used & is_last & (g == n_g - 1))
    def _():
        gather_wait_all()


def _experts(g_e, g_blk, g_rows, tok, yrow, hn2p, w_gate, b_gate, w_up, b_up, w_down, b_down, n_groups, n_yrows):
    nf = D_FF // FF_TILE
    bg3 = b_gate.reshape(N_EXPERTS, 1, D_FF)
    bu3 = b_up.reshape(N_EXPERTS, 1, D_FF)
    bd3 = b_down.reshape(N_EXPERTS, 1, D_MODEL)
    tok3 = tok.reshape(n_groups, 1, ROW_GROUP)
    yrow3 = yrow.reshape(n_groups, 1, ROW_GROUP)

    def ff(gi, fi, ge, gb, gn):
        return jnp.where(gn[gi] > 0, fi, nf - 1)

    grid_spec = pltpu.PrefetchScalarGridSpec(
        num_scalar_prefetch=3,
        grid=(n_groups, nf),
        in_specs=[
            pl.BlockSpec((1, 1, ROW_GROUP), lambda gi, fi, ge, gb, gn: (gb[gi], 0, 0), memory_space=pltpu.SMEM),
            pl.BlockSpec((1, 1, ROW_GROUP), lambda gi, fi, ge, gb, gn: (gb[jnp.minimum(gi + 1, n_groups - 1)], 0, 0),
                         memory_space=pltpu.SMEM),
            pl.BlockSpec((1, 1, ROW_GROUP), lambda gi, fi, ge, gb, gn: (gb[gi], 0, 0), memory_space=pltpu.SMEM),
            pl.BlockSpec(memory_space=pl.ANY),
            pl.BlockSpec((1, D_MODEL, FF_TILE), lambda gi, fi, ge, gb, gn: (ge[gi], 0, ff(gi, fi, ge, gb, gn))),
            pl.BlockSpec((1, 1, FF_TILE), lambda gi, fi, ge, gb, gn: (ge[gi], 0, ff(gi, fi, ge, gb, gn))),
            pl.BlockSpec((1, D_MODEL, FF_TILE), lambda gi, fi, ge, gb, gn: (ge[gi], 0, ff(gi, fi, ge, gb, gn))),
            pl.BlockSpec((1, 1, FF_TILE), lambda gi, fi, ge, gb, gn: (ge[gi], 0, ff(gi, fi, ge, gb, gn))),
            pl.BlockSpec((1, FF_TILE, D_MODEL), lambda gi, fi, ge, gb, gn: (ge[gi], ff(gi, fi, ge, gb, gn), 0)),
            pl.BlockSpec((1, 1, D_MODEL), lambda gi, fi, ge, gb, gn: (ge[gi], 0, 0)),
        ],
        out_specs=pl.BlockSpec(memory_space=pl.ANY),
        scratch_shapes=[
            pltpu.VMEM((ROW_GROUP * ROW_C, LANES), F32),
            pltpu.VMEM((ROW_GROUP, D_MODEL), BF16),
            pltpu.VMEM((ROW_GROUP, D_MODEL), F32),
            pltpu.VMEM((ROW_GROUP // SUB_ROWS, SUB_ROWS * ROW_C, LANES), F32),
            pltpu.SemaphoreType.DMA((ROW_GROUP // SUB_ROWS,)), pltpu.SemaphoreType.DMA(()),
        ],
    )
    return pl.pallas_call(
        _experts_kernel,
        grid_spec=grid_spec,
        out_shape=jax.ShapeDtypeStruct((n_yrows * ROW_C, LANES), F32),
        compiler_params=_cparams(("arbitrary", "arbitrary")),
        name="experts",
    )(g_e, g_blk, g_rows, tok3, tok3, yrow3, hn2p, w_gate, bg3, w_up, bu3, w_down, bd3)


CB_TM = 256


def _combine_kernel(h_ref, gt_ref, g_ref, y0_ref, y1_ref, y2_ref, y3_ref, o_ref, h2_ref):
    gt = gt_ref[...]
    ys = (y0_ref, y1_ref, y2_ref, y3_ref)
    for c in range(ROW_C):
        y = ys[0][pl.ds(c, CB_TM, stride=ROW_C), :] * gt[:, 0:1]
        for kk in range(1, TOP_K):
            y = y + ys[kk][pl.ds(c, CB_TM, stride=ROW_C), :] * gt[:, kk:kk + 1]
        h2_ref[:, c * LANES:(c + 1) * LANES] = h_ref[:, c * LANES:(c + 1) * LANES] + y
    h = h2_ref[...]
    o_ref[...] = h * lax.rsqrt(jnp.mean(h * h, axis=-1, keepdims=True) + NORM_EPS) * g_ref[...]


def _combine(h, gates, g_final, ybuf):
    T = h.shape[0]
    steps = T // CB_TM

    def yspec(kk):
        return pl.BlockSpec((CB_TM * ROW_C, LANES), lambda i: (kk * steps + i, 0))

    return pl.pallas_call(
        _combine_kernel,
        grid=(steps,),
        in_specs=[
            pl.BlockSpec((CB_TM, D_MODEL), lambda i: (i, 0)),
            pl.BlockSpec((CB_TM, TOP_K), lambda i: (i, 0)),
            pl.BlockSpec((1, D_MODEL), lambda i: (0, 0)),
            yspec(0), yspec(1), yspec(2), yspec(3),
        ],
        out_specs=pl.BlockSpec((CB_TM, D_MODEL), lambda i: (i, 0)),
        out_shape=jax.ShapeDtypeStruct((T, D_MODEL), F32),
        scratch_shapes=[pltpu.VMEM((CB_TM, D_MODEL), F32)],
        compiler_params=_cparams(("parallel",)),
        name="combine",
    )(h, gates, g_final, ybuf, ybuf, ybuf, ybuf)


def _group_tables(counts, top_i, rank, n_groups, T):
    A = T * TOP_K
    P = n_groups * ROW_GROUP
    counts = counts.reshape(N_EXPERTS)
    groups_e = (counts + ROW_GROUP - 1) // ROW_GROUP
    gend = jnp.cumsum(groups_e)
    gstart = gend - groups_e
    total = gend[-1]
    dest = ((gstart * ROW_GROUP)[top_i] + rank).astype(jnp.int32)
    gidx = jnp.arange(n_groups, dtype=jnp.int32)
    used = gidx < total
    last = jnp.maximum(total - 1, 0)
    gsafe = jnp.where(used, gidx, last)
    g_e = jnp.minimum(jnp.sum(gend[None, :] <= gsafe[:, None], axis=1), N_EXPERTS - 1).astype(jnp.int32)
    rows = jnp.clip(counts[g_e] - (gsafe - gstart[g_e]) * ROW_GROUP, 0, ROW_GROUP)
    g_rows = jnp.where(used, rows, 0).astype(jnp.int32)
    src = (jnp.arange(P, dtype=jnp.int32) % A).at[dest.reshape(A)].set(jnp.arange(A, dtype=jnp.int32))
    tok = src // TOP_K
    yrow = (src % TOP_K) * T + tok
    return g_e, gsafe.astype(jnp.int32), g_rows, tok, yrow


def kernel(x, positions, norm_mix_g, w_in, w_out, rel_bias, norm_ffn_g, router_w, router_b,
           w_gate, b_gate, w_up, b_up, w_down, b_down, norm_final_g):
    B, S, D = x.shape
    T = B * S
    depth = norm_mix_g.shape[0]
    assert depth == 1, "the combine kernel applies the final norm, so exactly one layer is supported"
    n_groups = (T * TOP_K) // ROW_GROUP + N_EXPERTS
    h = x.reshape(T, D)
    pos2 = positions.reshape(T, 1)
    for l in range(depth):
        proj = _in_proj(h, norm_mix_g[l].reshape(1, D), w_in[l].astype(BF16))
        table = _bias_table(rel_bias[l])
        ret = _retention(proj, pos2, B, S)
        att = _attention(proj, table, B, S)
        h, hn2, top_i, gates, rank, counts = _out_router(
            h, ret, att, w_out[l].astype(BF16), norm_ffn_g[l].reshape(1, D),
            router_w[l], router_b[l].reshape(1, N_EXPERTS))
        g_e, g_blk, g_rows, tok, yrow = _group_tables(counts, top_i, rank, n_groups, T)
        ybuf = _experts(g_e, g_blk, g_rows, tok, yrow, hn2, w_gate[l], b_gate[l], w_up[l], b_up[l],
                        w_down[l], b_down[l], n_groups, T * TOP_K)
        h = _combine(h, gates, norm_final_g.reshape(1, D), ybuf)
    return h.reshape(B, S, D)
```

```python
import functools
import math

import jax
import jax.numpy as jnp
import numpy as np
from jax import lax
from jax.experimental import pallas as pl
from jax.experimental.pallas import tpu as pltpu

D_MODEL = 2048
CHUNK = 64
RET_WIDTH = 1024
ATT_WIDTH = 1024
RET_HEAD_DIM = 256
RET_HEADS = RET_WIDTH // RET_HEAD_DIM
ATT_HEAD_DIM = 128
ATT_HEADS = ATT_WIDTH // ATT_HEAD_DIM
LEFT_CHUNKS = 8
MAX_REL = 256
REL_SIZE = MAX_REL + CHUNK
ROPE_BASE = 10000.0
N_EXPERTS = 32
TOP_K = 4
D_FF = D_MODEL
SWIGLU_LIMIT = 7.0
SWIGLU_ALPHA = 1.702
NORM_EPS = 1e-6
IN_COLS = 4 * RET_WIDTH + 3 * ATT_WIDTH
NEG_INF = -1e30

BF16 = jnp.bfloat16
F32 = jnp.float32

VMEM_LIMIT_BYTES = 56 * 1024 * 1024

ATT_QBLK = 256
ATT_LEFT = LEFT_CHUNKS * CHUNK
ATT_WIN = ATT_LEFT + ATT_QBLK
ROLL_W = 1024

ROW_GROUP = 1280
SUB_ROWS = 256
SUB_SHIFT = SUB_ROWS.bit_length() - 1
assert 1 << SUB_SHIFT == SUB_ROWS
FF_TILE = 256
LANES = 128
ROW_C = D_MODEL // LANES
GATHER_ROWS = ROW_GROUP // (D_FF // FF_TILE)
IDX_ALIGN = 1024
IDX_ALIGN_SHIFT = IDX_ALIGN.bit_length() - 1
IDX_LEN = IDX_ALIGN * (-(-(IDX_ALIGN - 1 + ROW_GROUP) // IDX_ALIGN))
IDX_SLOTS = 3
BIG_SUBS = (ROW_GROUP // SUB_ROWS - 1, ROW_GROUP // SUB_ROWS)


def _cparams(sem):
    return pltpu.CompilerParams(dimension_semantics=sem, vmem_limit_bytes=VMEM_LIMIT_BYTES)


def _in_proj_kernel(x_ref, g_ref, w_ref, o_ref, hn_ref):
    @pl.when(pl.program_id(1) == 0)
    def _():
        x = x_ref[...]
        ms = jnp.mean(x * x, axis=-1, keepdims=True)
        hn_ref[...] = (x * lax.rsqrt(ms + NORM_EPS) * g_ref[...]).astype(BF16)

    o_ref[...] = jnp.dot(hn_ref[...], w_ref[...], preferred_element_type=F32).astype(o_ref.dtype)


def _in_proj(x2, g, w_bf):
    T = x2.shape[0]
    tm, tn = 1024, 1024
    return pl.pallas_call(
        _in_proj_kernel,
        grid=(T // tm, IN_COLS // tn),
        in_specs=[
            pl.BlockSpec((tm, D_MODEL), lambda i, j: (i, 0)),
            pl.BlockSpec((1, D_MODEL), lambda i, j: (0, 0)),
            pl.BlockSpec((D_MODEL, tn), lambda i, j: (0, j)),
        ],
        out_specs=pl.BlockSpec((tm, tn), lambda i, j: (i, j)),
        out_shape=jax.ShapeDtypeStruct((T, IN_COLS), BF16),
        scratch_shapes=[pltpu.VMEM((tm, D_MODEL), BF16)],
        compiler_params=_cparams(("parallel", "arbitrary")),
        name="in_proj",
    )(x2, g, w_bf)


def _bias_table_kernel(sel_ref, rb_ref, o_ref):
    rb = rb_ref[...]
    hi = rb.astype(BF16)
    r1 = rb - hi.astype(F32)
    mid = r1.astype(BF16)
    lo = (r1 - mid.astype(F32)).astype(BF16)
    sel = sel_ref[...]
    frow = (jnp.dot(hi, sel, preferred_element_type=F32)
            + jnp.dot(mid, sel, preferred_element_type=F32)
            + jnp.dot(lo, sel, preferred_element_type=F32))
    i = lax.broadcasted_iota(jnp.int32, (ATT_QBLK, ATT_WIN), 0)
    c = lax.broadcasted_iota(jnp.int32, (ATT_QBLK, ATT_WIN), 1)
    qc = (i + ATT_LEFT) // CHUNK
    kc = c // CHUNK
    valid = (kc <= qc) & (kc >= qc - LEFT_CHUNKS)
    for h in range(ATT_HEADS):
        rows = jnp.broadcast_to(frow[h:h + 1, :], (ATT_QBLK, ROLL_W))
        rolled = pltpu.roll(rows, 0, 1, stride=1, stride_axis=0)
        o_ref[h] = jnp.where(valid, rolled[:, :ATT_WIN], NEG_INF)


def _bias_table(rel_bias):
    u = np.arange(-(ATT_QBLK - 1), ATT_WIN)
    idx = np.clip(ATT_LEFT - u, -(CHUNK - 1), MAX_REL) + (CHUNK - 1)
    sel = np.zeros((REL_SIZE, ROLL_W), np.float32)
    sel[idx, u % ROLL_W] = 1.0
    sel = jnp.asarray(sel, BF16)
    return pl.pallas_call(
        _bias_table_kernel,
        out_shape=jax.ShapeDtypeStruct((ATT_HEADS, ATT_QBLK, ATT_WIN), F32),
        compiler_params=pltpu.CompilerParams(vmem_limit_bytes=VMEM_LIMIT_BYTES),
        name="bias_table",
    )(sel, rel_bias)


RET_RB = 256


def _retention_kernel(pos_ref, invf_ref, q_ref, k_ref, v_ref, g_ref,
                      dintra_ref, dq_ref, dk_ref, dchunk_ref, o_ref, state_ref):
    @pl.when(pl.program_id(1) == 0)
    def _():
        state_ref[...] = jnp.zeros_like(state_ref)

    half = RET_HEAD_DIM // 2
    ang = pos_ref[...].astype(F32) * invf_ref[...]
    cos = jnp.cos(ang)
    sin = jnp.sin(ang)
    kscale = RET_HEAD_DIM ** -0.5

    for h in range(RET_HEADS):
        c0 = h * RET_HEAD_DIM
        q = q_ref[:, c0:c0 + RET_HEAD_DIM].astype(F32)
        k = k_ref[:, c0:c0 + RET_HEAD_DIM].astype(F32)
        q1, q2 = q[:, :half], q[:, half:]
        k1, k2 = k[:, :half], k[:, half:]
        qr = jnp.concatenate([q1 * cos - q2 * sin, q1 * sin + q2 * cos], axis=-1)
        kr = jnp.concatenate([k1 * cos - k2 * sin, k1 * sin + k2 * cos], axis=-1) * kscale
        d_intra = dintra_ref[h]
        d_q = dq_ref[h]
        d_k = dk_ref[h]
        d_c = dchunk_ref[h]
        for c in range(RET_RB // CHUNK):
            r0 = c * CHUNK
            qc = qr[r0:r0 + CHUNK]
            kc = kr[r0:r0 + CHUNK]
            vc = v_ref[r0:r0 + CHUNK, c0:c0 + RET_HEAD_DIM]
            qb = qc.astype(BF16)
            s = lax.dot_general(qb, kc.astype(BF16), (((1,), (1,)), ((), ())),
                                preferred_element_type=F32) * d_intra
            o = jnp.dot(s.astype(BF16), vc, preferred_element_type=F32)
            st = state_ref[h]
            o = o + jnp.dot((qc * d_q).astype(BF16), st.astype(BF16), preferred_element_type=F32)
            kv = lax.dot_general((kc * d_k).astype(BF16), vc, (((0,), (0,)), ((), ())),
                                 preferred_element_type=F32)
            state_ref[h] = st * d_c + kv
            o = o * lax.rsqrt(jnp.mean(o * o, axis=-1, keepdims=True) + NORM_EPS)
            g = g_ref[r0:r0 + CHUNK, c0:c0 + RET_HEAD_DIM].astype(F32)
            o_ref[r0:r0 + CHUNK, c0:c0 + RET_HEAD_DIM] = (o * (g * jax.nn.sigmoid(g))).astype(o_ref.dtype)


def _retention(proj, pos2, B, S):
    T = B * S
    nb = S // RET_RB
    half = RET_HEAD_DIM // 2
    L = CHUNK
    inv_freq = (1.0 / (ROPE_BASE ** (jnp.arange(half, dtype=F32) / half))).reshape(1, half)
    log_gamma = jnp.log(1.0 - jnp.exp(jnp.linspace(math.log(1.0 / 32), math.log(1.0 / 512), RET_HEADS)))
    n = jnp.arange(L, dtype=F32)
    d_intra = jnp.exp(jnp.abs(n[:, None] - n[None, :])[None] * log_gamma[:, None, None]).astype(F32)
    d_q = jnp.exp((n[None, :] + 1.0) * log_gamma[:, None]).astype(F32)[:, :, None]
    d_k = jnp.exp((L - 1.0 - n)[None, :] * log_gamma[:, None]).astype(F32)[:, :, None]
    d_chunk = jnp.exp(L * log_gamma).astype(F32).reshape(RET_HEADS, 1, 1)

    def col(j):
        return pl.BlockSpec((RET_RB, RET_WIDTH), lambda b, c: (b * nb + c, j))

    def full(shape):
        return pl.BlockSpec(shape, lambda b, c: (0,) * len(shape))

    return pl.pallas_call(
        _retention_kernel,
        grid=(B, nb),
        in_specs=[
            pl.BlockSpec((RET_RB, 1), lambda b, c: (b * nb + c, 0)),
            full((1, half)),
            col(0), col(1), col(2), col(3),
            full((RET_HEADS, L, L)), full((RET_HEADS, L, 1)), full((RET_HEADS, L, 1)),
            full((RET_HEADS, 1, 1)),
        ],
        out_specs=pl.BlockSpec((RET_RB, RET_WIDTH), lambda b, c: (b * nb + c, 0)),
        out_shape=jax.ShapeDtypeStruct((T, RET_WIDTH), BF16),
        scratch_shapes=[pltpu.VMEM((RET_HEADS, RET_HEAD_DIM, RET_HEAD_DIM), F32)],
        compiler_params=_cparams(("parallel", "arbitrary")),
        name="retention",
    )(pos2, inv_freq, proj, proj, proj, proj, d_intra, d_q, d_k, d_chunk)


def _attention_kernel(q_ref, k_ref, v_ref, tab_ref, o_ref):
    qb = pl.program_id(1)
    scale = ATT_HEAD_DIM ** -0.5

    def run(k_start, n_keys, col0):
        for h in range(ATT_HEADS):
            c0 = h * ATT_HEAD_DIM
            q = q_ref[:, c0:c0 + ATT_HEAD_DIM]
            k = k_ref[pl.ds(k_start, n_keys), c0:c0 + ATT_HEAD_DIM]
            v = v_ref[pl.ds(k_start, n_keys), c0:c0 + ATT_HEAD_DIM]
            s = lax.dot_general(q, k, (((1,), (1,)), ((), ())), preferred_element_type=F32)
            s = s * scale + tab_ref[h, :, col0:col0 + n_keys]
            m = jnp.max(s, axis=-1, keepdims=True)
            e = jnp.exp(s - m)
            o = jnp.dot(e.astype(BF16), v, preferred_element_type=F32) / jnp.sum(e, axis=-1, keepdims=True)
            o_ref[:, c0:c0 + ATT_HEAD_DIM] = o.astype(o_ref.dtype)

    @pl.when(qb == 0)
    def _():
        run(0, ATT_QBLK, ATT_LEFT)

    @pl.when(qb == 1)
    def _():
        run(0, 2 * ATT_QBLK, ATT_QBLK)

    @pl.when(qb >= 2)
    def _():
        run(pl.multiple_of((qb - 2) * ATT_QBLK, ATT_QBLK), ATT_WIN, 0)


def _attention(proj, table, B, S):
    T = B * S
    nq = S // ATT_QBLK
    return pl.pallas_call(
        _attention_kernel,
        grid=(B, nq),
        in_specs=[
            pl.BlockSpec((ATT_QBLK, ATT_WIDTH), lambda b, q: (b * nq + q, 4)),
            pl.BlockSpec((S, ATT_WIDTH), lambda b, q: (b, 5)),
            pl.BlockSpec((S, ATT_WIDTH), lambda b, q: (b, 6)),
            pl.BlockSpec((ATT_HEADS, ATT_QBLK, ATT_WIN), lambda b, q: (0, 0, 0)),
        ],
        out_specs=pl.BlockSpec((ATT_QBLK, ATT_WIDTH), lambda b, q: (b * nq + q, 0)),
        out_shape=jax.ShapeDtypeStruct((T, ATT_WIDTH), BF16),
        compiler_params=_cparams(("parallel", "arbitrary")),
        name="attention",
    )(proj, proj, proj, table)


OR_TM = 512


def _out_router_kernel(x_ref, ret_ref, att_ref, wo_ref, g_ref, rw_ref, rb_ref,
                       h_ref, hn_ref, ti_ref, gt_ref, cnt_ref, carry_ref):
    @pl.when(pl.program_id(0) == 0)
    def _():
        carry_ref[...] = jnp.zeros_like(carry_ref)

    mix = (jnp.dot(ret_ref[...], wo_ref[:RET_WIDTH, :], preferred_element_type=F32)
           + jnp.dot(att_ref[...], wo_ref[RET_WIDTH:, :], preferred_element_type=F32))
    h = x_ref[...] + mix
    h_ref[...] = h
    hn = h * lax.rsqrt(jnp.mean(h * h, axis=-1, keepdims=True) + NORM_EPS) * g_ref[...]
    hn_bf = hn.astype(BF16)
    for c in range(ROW_C):
        hn_ref[pl.ds(c, OR_TM, stride=ROW_C), :] = hn[:, c * LANES:(c + 1) * LANES]

    hn_lo = (hn - hn_bf.astype(F32)).astype(BF16)
    rw = rw_ref[...]
    rw_hi = rw.astype(BF16)
    rw_lo = (rw - rw_hi.astype(F32)).astype(BF16)
    logits = (jnp.dot(hn_bf, rw_hi, preferred_element_type=F32)
              + jnp.dot(hn_bf, rw_lo, preferred_element_type=F32)
              + jnp.dot(hn_lo, rw_hi, preferred_element_type=F32)) + rb_ref[...]

    lane = lax.broadcasted_iota(jnp.int32, (OR_TM, N_EXPERTS), 1)
    work = logits
    vals, idxs = [], []
    for _ in range(TOP_K):
        m = jnp.max(work, axis=-1, keepdims=True)
        idx = jnp.min(jnp.where(work == m, lane, N_EXPERTS), axis=-1, keepdims=True)
        vals.append(m)
        idxs.append(idx)
        work = jnp.where(lane == idx, -jnp.inf, work)
    es = [jnp.exp(v - vals[0]) for v in vals]
    den = es[0] + es[1] + es[2] + es[3]

    sel = jnp.zeros((OR_TM, N_EXPERTS), F32)
    for idx in idxs:
        sel = sel + (lane == idx).astype(F32)
    for kk in range(TOP_K):
        ti_ref[:, kk:kk + 1] = idxs[kk]
        gt_ref[:, kk:kk + 1] = es[kk] / den
    carry_ref[...] = carry_ref[...] + jnp.sum(sel, axis=0, keepdims=True)
    cnt_ref[...] = carry_ref[...].astype(jnp.int32)


def _out_router(x2, ret, att, wo_bf, g, rw, rb):
    T = x2.shape[0]
    tm = OR_TM
    row = lambda n: pl.BlockSpec((tm, n), lambda i: (i, 0))
    fix = lambda a, b: pl.BlockSpec((a, b), lambda i: (0, 0))
    return pl.pallas_call(
        _out_router_kernel,
        grid=(T // tm,),
        in_specs=[row(D_MODEL), row(RET_WIDTH), row(ATT_WIDTH), fix(D_MODEL, D_MODEL),
                  fix(1, D_MODEL), fix(D_MODEL, N_EXPERTS), fix(1, N_EXPERTS)],
        out_specs=[row(D_MODEL), pl.BlockSpec((tm * ROW_C, LANES), lambda i: (i, 0)),
                   row(TOP_K), row(TOP_K), fix(1, N_EXPERTS)],
        out_shape=[
            jax.ShapeDtypeStruct((T, D_MODEL), F32),
            jax.ShapeDtypeStruct((T * ROW_C, LANES), F32),
            jax.ShapeDtypeStruct((T, TOP_K), jnp.int32),
            jax.ShapeDtypeStruct((T, TOP_K), F32),
            jax.ShapeDtypeStruct((1, N_EXPERTS), jnp.int32),
        ],
        scratch_shapes=[pltpu.VMEM((1, N_EXPERTS), F32)],
        compiler_params=_cparams(("arbitrary",)),
        name="out_router",
    )(x2, ret, att, wo_bf, g, rw, rb)


def _experts_kernel(ge_ref, grows_ref, goff_ref, dense_hbm, hn_hbm,
                    wg_ref, bg_ref, wu_ref, bu_ref, wd_ref, bd_ref, y_hbm,
                    xg, xbf, acc, stage, idx_buf, sem_g, sem_s, sem_i):
    g = pl.program_id(0)
    f = pl.program_id(1)
    n_f = pl.num_programs(1)
    n_g = pl.num_programs(0)
    nrows = grows_ref[g]
    nsub = lax.shift_right_logical(nrows + (SUB_ROWS - 1), SUB_SHIFT)
    used = nrows > 0
    prev_used = (g > 0) & (grows_ref[jnp.maximum(g - 1, 0)] > 0)
    n_sub_max = ROW_GROUP // SUB_ROWS
    blk_rows = SUB_ROWS * ROW_C
    g_next = jnp.minimum(g + 1, n_g - 1)

    def idx_copies(gi):
        start = pl.multiple_of(lax.shift_right_logical(goff_ref[gi], IDX_ALIGN_SHIFT) * IDX_ALIGN, IDX_ALIGN)
        slot = lax.rem(gi, IDX_SLOTS)
        return [pltpu.make_async_copy(
                    dense_hbm.at[w, pl.ds(start, IDX_LEN)],
                    idx_buf.at[pl.ds(pl.multiple_of((slot * 2 + w) * IDX_LEN, IDX_LEN), IDX_LEN)],
                    sem_i.at[slot])
                for w in range(2)]

    def idx_base(gi, w):
        return (lax.rem(gi, IDX_SLOTS) * 2 + w) * IDX_LEN + (goff_ref[gi] & (IDX_ALIGN - 1))

    def gather_copy(batch, pos0, dst_row0, j):
        src0 = pl.multiple_of(idx_buf[pos0 + j], ROW_C)
        dst0 = pl.multiple_of(dst_row0 + j * ROW_C, ROW_C)
        return pltpu.make_async_copy(hn_hbm.at[pl.ds(src0, ROW_C)], xg.at[pl.ds(dst0, ROW_C)], sem_g.at[batch])

    def gather_wait_all():
        for b in range(ROW_GROUP // GATHER_ROWS):
            blk = xg.at[pl.ds(b * GATHER_ROWS * ROW_C, GATHER_ROWS * ROW_C)]
            pltpu.make_async_copy(blk, blk, sem_g.at[b]).wait()

    scatter_base = idx_base(g, 1)

    def scatter_copy(s, r0, j):
        src0 = pl.multiple_of(j * ROW_C, ROW_C)
        dst0 = pl.multiple_of(idx_buf[scatter_base + r0 + j], ROW_C)
        return pltpu.make_async_copy(stage.at[s, pl.ds(src0, ROW_C)], y_hbm.at[pl.ds(dst0, ROW_C)], sem_s)

    def scatter_wait_block(s):
        blk = stage.at[s]
        pltpu.make_async_copy(blk, blk, sem_s).wait()

    def stage_rows(s, val):
        for c in range(ROW_C):
            stage[s, pl.ds(c, SUB_ROWS, stride=ROW_C), :] = val[:, c * LANES:(c + 1) * LANES]

    def scatter_partial(s, r0):
        n_here = jnp.minimum(nrows - r0, SUB_ROWS)

        def start(j, _):
            scatter_copy(s, r0, j).start()
            return 0

        def wait(j, _):
            scatter_copy(s, r0, j).wait()
            return 0

        lax.fori_loop(0, n_here, start, 0)
        lax.fori_loop(0, n_here, wait, 0)

    @pl.when(used & (f == 0) & (g == 0))
    def _():
        for gi in range(2):
            for cp in idx_copies(gi):
                cp.start()
            for cp in idx_copies(gi):
                cp.wait()
        base0 = idx_base(0, 0)
        for b in range(ROW_GROUP // GATHER_ROWS):
            def start(j, _, b=b):
                gather_copy(b, base0 + b * GATHER_ROWS, b * GATHER_ROWS * ROW_C, j).start()
                return 0

            lax.fori_loop(0, GATHER_ROWS, start, 0)

    has_ahead = used & (g + 2 < n_g)

    @pl.when(has_ahead & (f == 0))
    def _():
        for cp in idx_copies(g + 2):
            cp.start()

    @pl.when(has_ahead & (f == n_f - 1))
    def _():
        for cp in idx_copies(g + 2):
            cp.wait()

    @pl.when((f == 0) & (used | prev_used))
    def _():
        gather_wait_all()

    @pl.when(used & (f == 0))
    def _():
        def unpack(s, _):
            r0 = pl.multiple_of(s * SUB_ROWS, SUB_ROWS)
            for c in range(ROW_C):
                v = xg[pl.ds(r0 * ROW_C + c, SUB_ROWS, stride=ROW_C), :]
                xbf[pl.ds(r0, SUB_ROWS), c * LANES:(c + 1) * LANES] = v.astype(BF16)
            acc[pl.ds(r0, SUB_ROWS), :] = jnp.broadcast_to(bd_ref[0], (SUB_ROWS, D_MODEL))
            return 0

        n_read = jnp.where(nsub == 1, 1, jnp.where(nsub <= BIG_SUBS[0], BIG_SUBS[0], BIG_SUBS[1]))
        lax.fori_loop(0, n_read, unpack, 0)

    def prefetch_next():
        pos0 = idx_base(g_next, 0) + f * GATHER_ROWS
        dst_row0 = f * (GATHER_ROWS * ROW_C)
        for j in range(GATHER_ROWS):
            gather_copy(f, pos0, dst_row0, j).start()

    def hidden(m):
        xb = xbf[0:m, :]
        gate = jnp.dot(xb, wg_ref[0].astype(BF16), preferred_element_type=F32) + bg_ref[0]
        gate = jnp.minimum(gate, SWIGLU_LIMIT)
        up = jnp.dot(xb, wu_ref[0].astype(BF16), preferred_element_type=F32) + bu_ref[0]
        up = jnp.clip(up, -SWIGLU_LIMIT, SWIGLU_LIMIT)
        return ((up + 1.0) * gate * jax.nn.sigmoid(SWIGLU_ALPHA * gate)).astype(BF16)

    def mlp_step(m):
        prefetch_next()
        acc[0:m, :] += jnp.dot(hidden(m), wd_ref[0].astype(BF16), preferred_element_type=F32)

    def mlp_last_big(n):
        prefetch_next()
        hid = hidden(n * SUB_ROWS)
        wd = wd_ref[0].astype(BF16)
        for s in range(n):
            r0 = s * SUB_ROWS
            out = acc[r0:r0 + SUB_ROWS, :] + jnp.dot(hid[r0:r0 + SUB_ROWS, :], wd, preferred_element_type=F32)
            stage_rows(s, out)
            if s < n - 1:
                for j in range(SUB_ROWS):
                    scatter_copy(s, r0, j).start()

    is_last = f == n_f - 1
    mid, top = BIG_SUBS
    big_last = is_last & ((nsub == mid) | (nsub == top))
    pl.when(nsub == 1)(functools.partial(mlp_step, SUB_ROWS))
    pl.when((nsub > 1) & (nsub <= mid) & jnp.logical_not(big_last))(functools.partial(mlp_step, mid * SUB_ROWS))
    pl.when((nsub > mid) & jnp.logical_not(big_last))(functools.partial(mlp_step, top * SUB_ROWS))
    for n in BIG_SUBS:
        pl.when(is_last & (nsub == n))(functools.partial(mlp_last_big, n))

    @pl.when(big_last)
    def _():
        scatter_partial(nsub - 1, pl.multiple_of((nsub - 1) * SUB_ROWS, SUB_ROWS))

        def wait(s, _):
            scatter_wait_block(s)
            return 0

        lax.fori_loop(0, nsub - 1, wait, 0)

    @pl.when(is_last & used & jnp.logical_not(big_last))
    def _():
        def emit(s, _):
            r0 = pl.multiple_of(s * SUB_ROWS, SUB_ROWS)
            stage_rows(0, acc[pl.ds(r0, SUB_ROWS), :])
            scatter_partial(0, r0)
            return 0

        lax.fori_loop(0, nsub, emit, 0)

    @pl.when(used & is_last & (g == n_g - 1))
    def _():
        gather_wait_all()


def _experts(g_e, g_rows, g_off, dense2d, hn2p, w_gate, b_gate, w_up, b_up, w_down, b_down, n_groups, n_yrows):
    nf = D_FF // FF_TILE
    bg3 = b_gate.reshape(N_EXPERTS, 1, D_FF)
    bu3 = b_up.reshape(N_EXPERTS, 1, D_FF)
    bd3 = b_down.reshape(N_EXPERTS, 1, D_MODEL)

    def ff(gi, fi, gn):
        return jnp.where(gn[gi] > 0, fi, nf - 1)

    grid_spec = pltpu.PrefetchScalarGridSpec(
        num_scalar_prefetch=3,
        grid=(n_groups, nf),
        in_specs=[
            pl.BlockSpec(memory_space=pl.ANY),
            pl.BlockSpec(memory_space=pl.ANY),
            pl.BlockSpec((1, D_MODEL, FF_TILE), lambda gi, fi, ge, gn, go: (ge[gi], 0, ff(gi, fi, gn))),
            pl.BlockSpec((1, 1, FF_TILE), lambda gi, fi, ge, gn, go: (ge[gi], 0, ff(gi, fi, gn))),
            pl.BlockSpec((1, D_MODEL, FF_TILE), lambda gi, fi, ge, gn, go: (ge[gi], 0, ff(gi, fi, gn))),
            pl.BlockSpec((1, 1, FF_TILE), lambda gi, fi, ge, gn, go: (ge[gi], 0, ff(gi, fi, gn))),
            pl.BlockSpec((1, FF_TILE, D_MODEL), lambda gi, fi, ge, gn, go: (ge[gi], ff(gi, fi, gn), 0)),
            pl.BlockSpec((1, 1, D_MODEL), lambda gi, fi, ge, gn, go: (ge[gi], 0, 0)),
        ],
        out_specs=pl.BlockSpec(memory_space=pl.ANY),
        scratch_shapes=[
            pltpu.VMEM((ROW_GROUP * ROW_C, LANES), F32),
            pltpu.VMEM((ROW_GROUP, D_MODEL), BF16),
            pltpu.VMEM((ROW_GROUP, D_MODEL), F32),
            pltpu.VMEM((ROW_GROUP // SUB_ROWS, SUB_ROWS * ROW_C, LANES), F32),
            pltpu.SMEM((IDX_SLOTS * 2 * IDX_LEN,), jnp.int32),
            pltpu.SemaphoreType.DMA((ROW_GROUP // GATHER_ROWS,)), pltpu.SemaphoreType.DMA(()),
            pltpu.SemaphoreType.DMA((IDX_SLOTS,)),
        ],
    )
    return pl.pallas_call(
        _experts_kernel,
        grid_spec=grid_spec,
        out_shape=jax.ShapeDtypeStruct((n_yrows * ROW_C, LANES), F32),
        compiler_params=_cparams(("arbitrary", "arbitrary")),
        name="experts",
    )(g_e, g_rows, g_off, dense2d, hn2p, w_gate, bg3, w_up, bu3, w_down, bd3)


CB_TM = 256


def _combine_kernel(h_ref, gt_ref, g_ref, y0_ref, y1_ref, y2_ref, y3_ref, o_ref, h2_ref):
    gt = gt_ref[...]
    ys = (y0_ref, y1_ref, y2_ref, y3_ref)
    for c in range(ROW_C):
        y = ys[0][pl.ds(c, CB_TM, stride=ROW_C), :] * gt[:, 0:1]
        for kk in range(1, TOP_K):
            y = y + ys[kk][pl.ds(c, CB_TM, stride=ROW_C), :] * gt[:, kk:kk + 1]
        h2_ref[:, c * LANES:(c + 1) * LANES] = h_ref[:, c * LANES:(c + 1) * LANES] + y
    h = h2_ref[...]
    o_ref[...] = h * lax.rsqrt(jnp.mean(h * h, axis=-1, keepdims=True) + NORM_EPS) * g_ref[...]


def _combine(h, gates, g_final, ybuf):
    T = h.shape[0]
    steps = T // CB_TM

    def yspec(kk):
        return pl.BlockSpec((CB_TM * ROW_C, LANES), lambda i: (kk * steps + i, 0))

    return pl.pallas_call(
        _combine_kernel,
        grid=(steps,),
        in_specs=[
            pl.BlockSpec((CB_TM, D_MODEL), lambda i: (i, 0)),
            pl.BlockSpec((CB_TM, TOP_K), lambda i: (i, 0)),
            pl.BlockSpec((1, D_MODEL), lambda i: (0, 0)),
            yspec(0), yspec(1), yspec(2), yspec(3),
        ],
        out_specs=pl.BlockSpec((CB_TM, D_MODEL), lambda i: (i, 0)),
        out_shape=jax.ShapeDtypeStruct((T, D_MODEL), F32),
        scratch_shapes=[pltpu.VMEM((CB_TM, D_MODEL), F32)],
        compiler_params=_cparams(("parallel",)),
        name="combine",
    )(h, gates, g_final, ybuf, ybuf, ybuf, ybuf)


def _group_tables(counts, top_i, n_groups, T):
    A = T * TOP_K
    counts = counts.reshape(N_EXPERTS)
    groups_e = (counts + ROW_GROUP - 1) // ROW_GROUP
    gend = jnp.cumsum(groups_e)
    gstart = gend - groups_e
    total = gend[-1]
    gidx = jnp.arange(n_groups, dtype=jnp.int32)
    used = gidx < total
    last = jnp.maximum(total - 1, 0)
    gsafe = jnp.where(used, gidx, last)
    g_e = jnp.minimum(jnp.sum(gend[None, :] <= gsafe[:, None], axis=1), N_EXPERTS - 1).astype(jnp.int32)
    rows = jnp.clip(counts[g_e] - (gsafe - gstart[g_e]) * ROW_GROUP, 0, ROW_GROUP)
    g_rows = jnp.where(used, rows, 0).astype(jnp.int32)
    cstart = jnp.cumsum(counts) - counts
    g_off = (cstart[g_e] + (gsafe - gstart[g_e]) * ROW_GROUP).astype(jnp.int32)
    a = jnp.arange(A, dtype=jnp.int32)
    dense = lax.sort(top_i.reshape(A) * A + a) % A
    tok = dense // TOP_K
    lists = jnp.stack([tok * ROW_C, ((dense % TOP_K) * T + tok) * ROW_C])
    dense2 = jnp.concatenate([lists, jnp.zeros((2, IDX_LEN), jnp.int32)], axis=1)
    return g_e, g_rows, g_off, dense2


def kernel(x, positions, norm_mix_g, w_in, w_out, rel_bias, norm_ffn_g, router_w, router_b,
           w_gate, b_gate, w_up, b_up, w_down, b_down, norm_final_g):
    B, S, D = x.shape
    T = B * S
    depth = norm_mix_g.shape[0]
    assert depth == 1, "the combine kernel applies the final norm, so exactly one layer is supported"
    n_groups = (T * TOP_K) // ROW_GROUP + N_EXPERTS
    h = x.reshape(T, D)
    pos2 = positions.reshape(T, 1)
    for l in range(depth):
        proj = _in_proj(h, norm_mix_g[l].reshape(1, D), w_in[l].astype(BF16))
        table = _bias_table(rel_bias[l])
        ret = _retention(proj, pos2, B, S)
        att = _attention(proj, table, B, S)
        h, hn2, top_i, gates, counts = _out_router(
            h, ret, att, w_out[l].astype(BF16), norm_ffn_g[l].reshape(1, D),
            router_w[l], router_b[l].reshape(1, N_EXPERTS))
        g_e, g_rows, g_off, dense2d = _group_tables(counts, top_i, n_groups, T)
        ybuf = _experts(g_e, g_rows, g_off, dense2d, hn2, w_gate[l], b_gate[l], w_up[l], b_up[l],
                        w_down[l], b_down[l], n_groups, T * TOP_K)
        h = _combine(h, gates, norm_final_g.reshape(1, D), ybuf)
    return h.reshape(B, S, D)
```

```python
import functools
import math

import jax
import jax.numpy as jnp
import numpy as np
from jax import lax
from jax.experimental import pallas as pl
from jax.experimental.pallas import tpu as pltpu

D_MODEL = 2048
CHUNK = 64
RET_WIDTH = 1024
ATT_WIDTH = 1024
RET_HEAD_DIM = 256
RET_HEADS = RET_WIDTH // RET_HEAD_DIM
ATT_HEAD_DIM = 128
ATT_HEADS = ATT_WIDTH // ATT_HEAD_DIM
LEFT_CHUNKS = 8
MAX_REL = 256
REL_SIZE = MAX_REL + CHUNK
ROPE_BASE = 10000.0
N_EXPERTS = 32
TOP_K = 4
D_FF = D_MODEL
SWIGLU_LIMIT = 7.0
SWIGLU_ALPHA = 1.702
NORM_EPS = 1e-6
IN_COLS = 4 * RET_WIDTH + 3 * ATT_WIDTH
NEG_INF = -1e30

BF16 = jnp.bfloat16
F32 = jnp.float32

VMEM_LIMIT_BYTES = 56 * 1024 * 1024

ATT_QBLK = 256
ATT_LEFT = LEFT_CHUNKS * CHUNK
ATT_WIN = ATT_LEFT + ATT_QBLK
ROLL_W = 1024

ROW_GROUP = 1152
SUB_ROWS = 128
SUB_SHIFT = SUB_ROWS.bit_length() - 1
assert 1 << SUB_SHIFT == SUB_ROWS
FF_TILE = 256
LANES = 128
ROW_C = D_MODEL // LANES
GATHER_ROWS = ROW_GROUP // (D_FF // FF_TILE)
IDX_ALIGN = 1024
IDX_ALIGN_SHIFT = IDX_ALIGN.bit_length() - 1
IDX_LEN = IDX_ALIGN * (-(-(IDX_ALIGN - 1 + ROW_GROUP) // IDX_ALIGN))
IDX_SLOTS = 3
BIG_SUBS = (ROW_GROUP // SUB_ROWS - 1, ROW_GROUP // SUB_ROWS)


def _cparams(sem):
    return pltpu.CompilerParams(dimension_semantics=sem, vmem_limit_bytes=VMEM_LIMIT_BYTES)


def _in_proj_kernel(x_ref, g_ref, w_ref, o_ref, hn_ref):
    @pl.when(pl.program_id(1) == 0)
    def _():
        x = x_ref[...]
        ms = jnp.mean(x * x, axis=-1, keepdims=True)
        hn_ref[...] = (x * lax.rsqrt(ms + NORM_EPS) * g_ref[...]).astype(BF16)

    o_ref[...] = jnp.dot(hn_ref[...], w_ref[...], preferred_element_type=F32).astype(o_ref.dtype)


def _in_proj(x2, g, w_bf):
    T = x2.shape[0]
    tm, tn = 1024, 1024
    return pl.pallas_call(
        _in_proj_kernel,
        grid=(T // tm, IN_COLS // tn),
        in_specs=[
            pl.BlockSpec((tm, D_MODEL), lambda i, j: (i, 0)),
            pl.BlockSpec((1, D_MODEL), lambda i, j: (0, 0)),
            pl.BlockSpec((D_MODEL, tn), lambda i, j: (0, j)),
        ],
        out_specs=pl.BlockSpec((tm, tn), lambda i, j: (i, j)),
        out_shape=jax.ShapeDtypeStruct((T, IN_COLS), BF16),
        scratch_shapes=[pltpu.VMEM((tm, D_MODEL), BF16)],
        compiler_params=_cparams(("parallel", "arbitrary")),
        name="in_proj",
    )(x2, g, w_bf)


def _bias_table_kernel(sel_ref, rb_ref, o_ref):
    rb = rb_ref[...]
    hi = rb.astype(BF16)
    r1 = rb - hi.astype(F32)
    mid = r1.astype(BF16)
    lo = (r1 - mid.astype(F32)).astype(BF16)
    sel = sel_ref[...]
    frow = (jnp.dot(hi, sel, preferred_element_type=F32)
            + jnp.dot(mid, sel, preferred_element_type=F32)
            + jnp.dot(lo, sel, preferred_element_type=F32))
    i = lax.broadcasted_iota(jnp.int32, (ATT_QBLK, ATT_WIN), 0)
    c = lax.broadcasted_iota(jnp.int32, (ATT_QBLK, ATT_WIN), 1)
    qc = (i + ATT_LEFT) // CHUNK
    kc = c // CHUNK
    valid = (kc <= qc) & (kc >= qc - LEFT_CHUNKS)
    for h in range(ATT_HEADS):
        rows = jnp.broadcast_to(frow[h:h + 1, :], (ATT_QBLK, ROLL_W))
        rolled = pltpu.roll(rows, 0, 1, stride=1, stride_axis=0)
        o_ref[h] = jnp.where(valid, rolled[:, :ATT_WIN], NEG_INF)


def _bias_table(rel_bias):
    u = np.arange(-(ATT_QBLK - 1), ATT_WIN)
    idx = np.clip(ATT_LEFT - u, -(CHUNK - 1), MAX_REL) + (CHUNK - 1)
    sel = np.zeros((REL_SIZE, ROLL_W), np.float32)
    sel[idx, u % ROLL_W] = 1.0
    sel = jnp.asarray(sel, BF16)
    return pl.pallas_call(
        _bias_table_kernel,
        out_shape=jax.ShapeDtypeStruct((ATT_HEADS, ATT_QBLK, ATT_WIN), F32),
        compiler_params=pltpu.CompilerParams(vmem_limit_bytes=VMEM_LIMIT_BYTES),
        name="bias_table",
    )(sel, rel_bias)


RET_RB = 256


def _retention_kernel(pos_ref, invf_ref, q_ref, k_ref, v_ref, g_ref,
                      dintra_ref, dq_ref, dk_ref, dchunk_ref, o_ref, state_ref):
    @pl.when(pl.program_id(1) == 0)
    def _():
        state_ref[...] = jnp.zeros_like(state_ref)

    half = RET_HEAD_DIM // 2
    ang = pos_ref[...].astype(F32) * invf_ref[...]
    cos = jnp.cos(ang)
    sin = jnp.sin(ang)
    kscale = RET_HEAD_DIM ** -0.5

    for h in range(RET_HEADS):
        c0 = h * RET_HEAD_DIM
        q = q_ref[:, c0:c0 + RET_HEAD_DIM].astype(F32)
        k = k_ref[:, c0:c0 + RET_HEAD_DIM].astype(F32)
        q1, q2 = q[:, :half], q[:, half:]
        k1, k2 = k[:, :half], k[:, half:]
        qr = jnp.concatenate([q1 * cos - q2 * sin, q1 * sin + q2 * cos], axis=-1)
        kr = jnp.concatenate([k1 * cos - k2 * sin, k1 * sin + k2 * cos], axis=-1) * kscale
        d_intra = dintra_ref[h]
        d_q = dq_ref[h]
        d_k = dk_ref[h]
        d_c = dchunk_ref[h]
        for c in range(RET_RB // CHUNK):
            r0 = c * CHUNK
            qc = qr[r0:r0 + CHUNK]
            kc = kr[r0:r0 + CHUNK]
            vc = v_ref[r0:r0 + CHUNK, c0:c0 + RET_HEAD_DIM]
            qb = qc.astype(BF16)
            s = lax.dot_general(qb, kc.astype(BF16), (((1,), (1,)), ((), ())),
                                preferred_element_type=F32) * d_intra
            o = jnp.dot(s.astype(BF16), vc, preferred_element_type=F32)
            st = state_ref[h]
            o = o + jnp.dot((qc * d_q).astype(BF16), st.astype(BF16), preferred_element_type=F32)
            kv = lax.dot_general((kc * d_k).astype(BF16), vc, (((0,), (0,)), ((), ())),
                                 preferred_element_type=F32)
            state_ref[h] = st * d_c + kv
            o = o * lax.rsqrt(jnp.mean(o * o, axis=-1, keepdims=True) + NORM_EPS)
            g = g_ref[r0:r0 + CHUNK, c0:c0 + RET_HEAD_DIM].astype(F32)
            o_ref[r0:r0 + CHUNK, c0:c0 + RET_HEAD_DIM] = (o * (g * jax.nn.sigmoid(g))).astype(o_ref.dtype)


def _retention(proj, pos2, B, S):
    T = B * S
    nb = S // RET_RB
    half = RET_HEAD_DIM // 2
    L = CHUNK
    inv_freq = (1.0 / (ROPE_BASE ** (jnp.arange(half, dtype=F32) / half))).reshape(1, half)
    log_gamma = jnp.log(1.0 - jnp.exp(jnp.linspace(math.log(1.0 / 32), math.log(1.0 / 512), RET_HEADS)))
    n = jnp.arange(L, dtype=F32)
    d_intra = jnp.exp(jnp.abs(n[:, None] - n[None, :])[None] * log_gamma[:, None, None]).astype(F32)
    d_q = jnp.exp((n[None, :] + 1.0) * log_gamma[:, None]).astype(F32)[:, :, None]
    d_k = jnp.exp((L - 1.0 - n)[None, :] * log_gamma[:, None]).astype(F32)[:, :, None]
    d_chunk = jnp.exp(L * log_gamma).astype(F32).reshape(RET_HEADS, 1, 1)

    def col(j):
        return pl.BlockSpec((RET_RB, RET_WIDTH), lambda b, c: (b * nb + c, j))

    def full(shape):
        return pl.BlockSpec(shape, lambda b, c: (0,) * len(shape))

    return pl.pallas_call(
        _retention_kernel,
        grid=(B, nb),
        in_specs=[
            pl.BlockSpec((RET_RB, 1), lambda b, c: (b * nb + c, 0)),
            full((1, half)),
            col(0), col(1), col(2), col(3),
            full((RET_HEADS, L, L)), full((RET_HEADS, L, 1)), full((RET_HEADS, L, 1)),
            full((RET_HEADS, 1, 1)),
        ],
        out_specs=pl.BlockSpec((RET_RB, RET_WIDTH), lambda b, c: (b * nb + c, 0)),
        out_shape=jax.ShapeDtypeStruct((T, RET_WIDTH), BF16),
        scratch_shapes=[pltpu.VMEM((RET_HEADS, RET_HEAD_DIM, RET_HEAD_DIM), F32)],
        compiler_params=_cparams(("parallel", "arbitrary")),
        name="retention",
    )(pos2, inv_freq, proj, proj, proj, proj, d_intra, d_q, d_k, d_chunk)


def _attention_kernel(q_ref, k_ref, v_ref, tab_ref, o_ref):
    qb = pl.program_id(1)
    scale = ATT_HEAD_DIM ** -0.5

    def run(k_start, n_keys, col0):
        for h in range(ATT_HEADS):
            c0 = h * ATT_HEAD_DIM
            q = q_ref[:, c0:c0 + ATT_HEAD_DIM]
            k = k_ref[pl.ds(k_start, n_keys), c0:c0 + ATT_HEAD_DIM]
            v = v_ref[pl.ds(k_start, n_keys), c0:c0 + ATT_HEAD_DIM]
            s = lax.dot_general(q, k, (((1,), (1,)), ((), ())), preferred_element_type=F32)
            s = s * scale + tab_ref[h, :, col0:col0 + n_keys]
            m = jnp.max(s, axis=-1, keepdims=True)
            e = jnp.exp(s - m)
            o = jnp.dot(e.astype(BF16), v, preferred_element_type=F32) / jnp.sum(e, axis=-1, keepdims=True)
            o_ref[:, c0:c0 + ATT_HEAD_DIM] = o.astype(o_ref.dtype)

    @pl.when(qb == 0)
    def _():
        run(0, ATT_QBLK, ATT_LEFT)

    @pl.when(qb == 1)
    def _():
        run(0, 2 * ATT_QBLK, ATT_QBLK)

    @pl.when(qb >= 2)
    def _():
        run(pl.multiple_of((qb - 2) * ATT_QBLK, ATT_QBLK), ATT_WIN, 0)


def _attention(proj, table, B, S):
    T = B * S
    nq = S // ATT_QBLK
    return pl.pallas_call(
        _attention_kernel,
        grid=(B, nq),
        in_specs=[
            pl.BlockSpec((ATT_QBLK, ATT_WIDTH), lambda b, q: (b * nq + q, 4)),
            pl.BlockSpec((S, ATT_WIDTH), lambda b, q: (b, 5)),
            pl.BlockSpec((S, ATT_WIDTH), lambda b, q: (b, 6)),
            pl.BlockSpec((ATT_HEADS, ATT_QBLK, ATT_WIN), lambda b, q: (0, 0, 0)),
        ],
        out_specs=pl.BlockSpec((ATT_QBLK, ATT_WIDTH), lambda b, q: (b * nq + q, 0)),
        out_shape=jax.ShapeDtypeStruct((T, ATT_WIDTH), BF16),
        compiler_params=_cparams(("parallel", "arbitrary")),
        name="attention",
    )(proj, proj, proj, table)


OR_TM = 512


def _out_router_kernel(x_ref, ret_ref, att_ref, wo_ref, g_ref, rw_ref, rb_ref,
                       h_ref, hn_ref, ti_ref, gt_ref, cnt_ref, carry_ref):
    @pl.when(pl.program_id(0) == 0)
    def _():
        carry_ref[...] = jnp.zeros_like(carry_ref)

    mix = (jnp.dot(ret_ref[...], wo_ref[:RET_WIDTH, :], preferred_element_type=F32)
           + jnp.dot(att_ref[...], wo_ref[RET_WIDTH:, :], preferred_element_type=F32))
    h = x_ref[...] + mix
    h_ref[...] = h
    hn = h * lax.rsqrt(jnp.mean(h * h, axis=-1, keepdims=True) + NORM_EPS) * g_ref[...]
    hn_bf = hn.astype(BF16)
    for c in range(ROW_C):
        hn_ref[pl.ds(c, OR_TM, stride=ROW_C), :] = hn[:, c * LANES:(c + 1) * LANES]

    hn_lo = (hn - hn_bf.astype(F32)).astype(BF16)
    rw = rw_ref[...]
    rw_hi = rw.astype(BF16)
    rw_lo = (rw - rw_hi.astype(F32)).astype(BF16)
    logits = (jnp.dot(hn_bf, rw_hi, preferred_element_type=F32)
              + jnp.dot(hn_bf, rw_lo, preferred_element_type=F32)
              + jnp.dot(hn_lo, rw_hi, preferred_element_type=F32)) + rb_ref[...]

    lane = lax.broadcasted_iota(jnp.int32, (OR_TM, N_EXPERTS), 1)
    work = logits
    vals, idxs = [], []
    for _ in range(TOP_K):
        m = jnp.max(work, axis=-1, keepdims=True)
        idx = jnp.min(jnp.where(work == m, lane, N_EXPERTS), axis=-1, keepdims=True)
        vals.append(m)
        idxs.append(idx)
        work = jnp.where(lane == idx, -jnp.inf, work)
    es = [jnp.exp(v - vals[0]) for v in vals]
    den = es[0] + es[1] + es[2] + es[3]

    sel = jnp.zeros((OR_TM, N_EXPERTS), F32)
    for idx in idxs:
        sel = sel + (lane == idx).astype(F32)
    for kk in range(TOP_K):
        ti_ref[:, kk:kk + 1] = idxs[kk]
        gt_ref[:, kk:kk + 1] = es[kk] / den
    carry_ref[...] = carry_ref[...] + jnp.sum(sel, axis=0, keepdims=True)
    cnt_ref[...] = carry_ref[...].astype(jnp.int32)


def _out_router(x2, ret, att, wo_bf, g, rw, rb):
    T = x2.shape[0]
    tm = OR_TM
    row = lambda n: pl.BlockSpec((tm, n), lambda i: (i, 0))
    fix = lambda a, b: pl.BlockSpec((a, b), lambda i: (0, 0))
    return pl.pallas_call(
        _out_router_kernel,
        grid=(T // tm,),
        in_specs=[row(D_MODEL), row(RET_WIDTH), row(ATT_WIDTH), fix(D_MODEL, D_MODEL),
                  fix(1, D_MODEL), fix(D_MODEL, N_EXPERTS), fix(1, N_EXPERTS)],
        out_specs=[row(D_MODEL), pl.BlockSpec((tm * ROW_C, LANES), lambda i: (i, 0)),
                   row(TOP_K), row(TOP_K), fix(1, N_EXPERTS)],
        out_shape=[
            jax.ShapeDtypeStruct((T, D_MODEL), F32),
            jax.ShapeDtypeStruct((T * ROW_C, LANES), F32),
            jax.ShapeDtypeStruct((T, TOP_K), jnp.int32),
            jax.ShapeDtypeStruct((T, TOP_K), F32),
            jax.ShapeDtypeStruct((1, N_EXPERTS), jnp.int32),
        ],
        scratch_shapes=[pltpu.VMEM((1, N_EXPERTS), F32)],
        compiler_params=_cparams(("arbitrary",)),
        name="out_router",
    )(x2, ret, att, wo_bf, g, rw, rb)


def _experts_kernel(ge_ref, grows_ref, goff_ref, dense_hbm, hn_hbm,
                    wg_ref, bg_ref, wu_ref, bu_ref, wd_ref, bd_ref, y_hbm,
                    xg, xbf, acc, stage, idx_buf, sem_g, sem_s, sem_i):
    g = pl.program_id(0)
    f = pl.program_id(1)
    n_f = pl.num_programs(1)
    n_g = pl.num_programs(0)
    nrows = grows_ref[g]
    nsub = lax.shift_right_logical(nrows + (SUB_ROWS - 1), SUB_SHIFT)
    used = nrows > 0
    prev_used = (g > 0) & (grows_ref[jnp.maximum(g - 1, 0)] > 0)
    g_next = jnp.minimum(g + 1, n_g - 1)

    def idx_copies(gi):
        start = pl.multiple_of(lax.shift_right_logical(goff_ref[gi], IDX_ALIGN_SHIFT) * IDX_ALIGN, IDX_ALIGN)
        slot = lax.rem(gi, IDX_SLOTS)
        return [pltpu.make_async_copy(
                    dense_hbm.at[w, pl.ds(start, IDX_LEN)],
                    idx_buf.at[pl.ds(pl.multiple_of((slot * 2 + w) * IDX_LEN, IDX_LEN), IDX_LEN)],
                    sem_i.at[slot])
                for w in range(2)]

    def idx_base(gi, w):
        return (lax.rem(gi, IDX_SLOTS) * 2 + w) * IDX_LEN + (goff_ref[gi] & (IDX_ALIGN - 1))

    def gather_copy(batch, pos0, dst_row0, j):
        src0 = pl.multiple_of(idx_buf[pos0 + j], ROW_C)
        dst0 = pl.multiple_of(dst_row0 + j * ROW_C, ROW_C)
        return pltpu.make_async_copy(hn_hbm.at[pl.ds(src0, ROW_C)], xg.at[pl.ds(dst0, ROW_C)], sem_g.at[batch])

    def gather_wait_all():
        for b in range(ROW_GROUP // GATHER_ROWS):
            blk = xg.at[pl.ds(b * GATHER_ROWS * ROW_C, GATHER_ROWS * ROW_C)]
            pltpu.make_async_copy(blk, blk, sem_g.at[b]).wait()

    scatter_base = idx_base(g, 1)

    def scatter_copy(s, r0, j):
        src0 = pl.multiple_of(j * ROW_C, ROW_C)
        dst0 = pl.multiple_of(idx_buf[scatter_base + r0 + j], ROW_C)
        return pltpu.make_async_copy(stage.at[s, pl.ds(src0, ROW_C)], y_hbm.at[pl.ds(dst0, ROW_C)], sem_s)

    def scatter_wait_block(s):
        blk = stage.at[s]
        pltpu.make_async_copy(blk, blk, sem_s).wait()

    def stage_rows(s, val):
        for c in range(ROW_C):
            stage[s, pl.ds(c, SUB_ROWS, stride=ROW_C), :] = val[:, c * LANES:(c + 1) * LANES]

    def scatter_partial(s, r0):
        n_here = jnp.minimum(nrows - r0, SUB_ROWS)

        def start(j, _):
            scatter_copy(s, r0, j).start()
            return 0

        def wait(j, _):
            scatter_copy(s, r0, j).wait()
            return 0

        lax.fori_loop(0, n_here, start, 0)
        lax.fori_loop(0, n_here, wait, 0)

    @pl.when(used & (f == 0) & (g == 0))
    def _():
        for gi in range(2):
            for cp in idx_copies(gi):
                cp.start()
            for cp in idx_copies(gi):
                cp.wait()
        base0 = idx_base(0, 0)
        for b in range(ROW_GROUP // GATHER_ROWS):
            def start(j, _, b=b):
                gather_copy(b, base0 + b * GATHER_ROWS, b * GATHER_ROWS * ROW_C, j).start()
                return 0

            lax.fori_loop(0, GATHER_ROWS, start, 0)

    has_ahead = used & (g + 2 < n_g)

    @pl.when(has_ahead & (f == 0))
    def _():
        for cp in idx_copies(g + 2):
            cp.start()

    @pl.when(has_ahead & (f == n_f - 1))
    def _():
        for cp in idx_copies(g + 2):
            cp.wait()

    @pl.when((f == 0) & (used | prev_used))
    def _():
        gather_wait_all()

    @pl.when(used & (f == 0))
    def _():
        def unpack(s, _):
            r0 = pl.multiple_of(s * SUB_ROWS, SUB_ROWS)
            for c in range(ROW_C):
                v = xg[pl.ds(r0 * ROW_C + c, SUB_ROWS, stride=ROW_C), :]
                xbf[pl.ds(r0, SUB_ROWS), c * LANES:(c + 1) * LANES] = v.astype(BF16)
            acc[pl.ds(r0, SUB_ROWS), :] = jnp.broadcast_to(bd_ref[0], (SUB_ROWS, D_MODEL))
            return 0

        n_read = jnp.where(nsub == 1, 1, jnp.where(nsub <= BIG_SUBS[0], BIG_SUBS[0], BIG_SUBS[1]))
        lax.fori_loop(0, n_read, unpack, 0)

    def prefetch_next():
        pos0 = idx_base(g_next, 0) + f * GATHER_ROWS
        dst_row0 = f * (GATHER_ROWS * ROW_C)
        for j in range(GATHER_ROWS):
            gather_copy(f, pos0, dst_row0, j).start()

    def hidden(m):
        xb = xbf[0:m, :]
        gate = jnp.dot(xb, wg_ref[0].astype(BF16), preferred_element_type=F32) + bg_ref[0]
        gate = jnp.minimum(gate, SWIGLU_LIMIT)
        up = jnp.dot(xb, wu_ref[0].astype(BF16), preferred_element_type=F32) + bu_ref[0]
        up = jnp.clip(up, -SWIGLU_LIMIT, SWIGLU_LIMIT)
        return ((up + 1.0) * gate * jax.nn.sigmoid(SWIGLU_ALPHA * gate)).astype(BF16)

    def mlp_step(m):
        prefetch_next()
        acc[0:m, :] += jnp.dot(hidden(m), wd_ref[0].astype(BF16), preferred_element_type=F32)

    def mlp_last_big(n):
        prefetch_next()
        hid = hidden(n * SUB_ROWS)
        wd = wd_ref[0].astype(BF16)
        for s in range(n):
            r0 = s * SUB_ROWS
            out = acc[r0:r0 + SUB_ROWS, :] + jnp.dot(hid[r0:r0 + SUB_ROWS, :], wd, preferred_element_type=F32)
            stage_rows(s, out)
            if s < n - 1:
                for j in range(SUB_ROWS):
                    scatter_copy(s, r0, j).start()

    is_last = f == n_f - 1
    mid, top = BIG_SUBS
    big_last = is_last & ((nsub == mid) | (nsub == top))
    pl.when(nsub == 1)(functools.partial(mlp_step, SUB_ROWS))
    pl.when((nsub > 1) & (nsub <= mid) & jnp.logical_not(big_last))(functools.partial(mlp_step, mid * SUB_ROWS))
    pl.when((nsub > mid) & jnp.logical_not(big_last))(functools.partial(mlp_step, top * SUB_ROWS))
    for n in BIG_SUBS:
        pl.when(is_last & (nsub == n))(functools.partial(mlp_last_big, n))

    @pl.when(big_last)
    def _():
        scatter_partial(nsub - 1, pl.multiple_of((nsub - 1) * SUB_ROWS, SUB_ROWS))

        def wait(s, _):
            scatter_wait_block(s)
            return 0

        lax.fori_loop(0, nsub - 1, wait, 0)

    @pl.when(is_last & used & jnp.logical_not(big_last))
    def _():
        def emit(s, _):
            r0 = pl.multiple_of(s * SUB_ROWS, SUB_ROWS)
            stage_rows(0, acc[pl.ds(r0, SUB_ROWS), :])
            scatter_partial(0, r0)
            return 0

        lax.fori_loop(0, nsub, emit, 0)

    @pl.when(used & is_last & (g == n_g - 1))
    def _():
        gather_wait_all()


def _experts(g_e, g_rows, g_off, dense2, hn_rows, w_gate, b_gate, w_up, b_up, w_down, b_down, n_groups, n_yrows):
    nf = D_FF // FF_TILE
    bg3 = b_gate.reshape(N_EXPERTS, 1, D_FF)
    bu3 = b_up.reshape(N_EXPERTS, 1, D_FF)
    bd3 = b_down.reshape(N_EXPERTS, 1, D_MODEL)

    def ff(gi, fi, gn):
        return jnp.where(gn[gi] > 0, fi, nf - 1)

    grid_spec = pltpu.PrefetchScalarGridSpec(
        num_scalar_prefetch=3,
        grid=(n_groups, nf),
        in_specs=[
            pl.BlockSpec(memory_space=pl.ANY),
            pl.BlockSpec(memory_space=pl.ANY),
            pl.BlockSpec((1, D_MODEL, FF_TILE), lambda gi, fi, ge, gn, go: (ge[gi], 0, ff(gi, fi, gn))),
            pl.BlockSpec((1, 1, FF_TILE), lambda gi, fi, ge, gn, go: (ge[gi], 0, ff(gi, fi, gn))),
            pl.BlockSpec((1, D_MODEL, FF_TILE), lambda gi, fi, ge, gn, go: (ge[gi], 0, ff(gi, fi, gn))),
            pl.BlockSpec((1, 1, FF_TILE), lambda gi, fi, ge, gn, go: (ge[gi], 0, ff(gi, fi, gn))),
            pl.BlockSpec((1, FF_TILE, D_MODEL), lambda gi, fi, ge, gn, go: (ge[gi], ff(gi, fi, gn), 0)),
            pl.BlockSpec((1, 1, D_MODEL), lambda gi, fi, ge, gn, go: (ge[gi], 0, 0)),
        ],
        out_specs=pl.BlockSpec(memory_space=pl.ANY),
        scratch_shapes=[
            pltpu.VMEM((ROW_GROUP * ROW_C, LANES), F32),
            pltpu.VMEM((ROW_GROUP, D_MODEL), BF16),
            pltpu.VMEM((ROW_GROUP, D_MODEL), F32),
            pltpu.VMEM((ROW_GROUP // SUB_ROWS, SUB_ROWS * ROW_C, LANES), F32),
            pltpu.SMEM((IDX_SLOTS * 2 * IDX_LEN,), jnp.int32),
            pltpu.SemaphoreType.DMA((ROW_GROUP // GATHER_ROWS,)), pltpu.SemaphoreType.DMA(()),
            pltpu.SemaphoreType.DMA((IDX_SLOTS,)),
        ],
    )
    return pl.pallas_call(
        _experts_kernel,
        grid_spec=grid_spec,
        out_shape=jax.ShapeDtypeStruct((n_yrows * ROW_C, LANES), F32),
        compiler_params=_cparams(("arbitrary", "arbitrary")),
        name="experts",
    )(g_e, g_rows, g_off, dense2, hn_rows, w_gate, bg3, w_up, bu3, w_down, bd3)


CB_TM = 256


def _combine_kernel(h_ref, gt_ref, g_ref, y0_ref, y1_ref, y2_ref, y3_ref, o_ref, h2_ref):
    gt = gt_ref[...]
    ys = (y0_ref, y1_ref, y2_ref, y3_ref)
    for c in range(ROW_C):
        y = ys[0][pl.ds(c, CB_TM, stride=ROW_C), :] * gt[:, 0:1]
        for kk in range(1, TOP_K):
            y = y + ys[kk][pl.ds(c, CB_TM, stride=ROW_C), :] * gt[:, kk:kk + 1]
        h2_ref[:, c * LANES:(c + 1) * LANES] = h_ref[:, c * LANES:(c + 1) * LANES] + y
    h = h2_ref[...]
    o_ref[...] = h * lax.rsqrt(jnp.mean(h * h, axis=-1, keepdims=True) + NORM_EPS) * g_ref[...]


def _combine(h, gates, g_final, ybuf):
    T = h.shape[0]
    steps = T // CB_TM

    def yspec(kk):
        return pl.BlockSpec((CB_TM * ROW_C, LANES), lambda i: (kk * steps + i, 0))

    return pl.pallas_call(
        _combine_kernel,
        grid=(steps,),
        in_specs=[
            pl.BlockSpec((CB_TM, D_MODEL), lambda i: (i, 0)),
            pl.BlockSpec((CB_TM, TOP_K), lambda i: (i, 0)),
            pl.BlockSpec((1, D_MODEL), lambda i: (0, 0)),
            yspec(0), yspec(1), yspec(2), yspec(3),
        ],
        out_specs=pl.BlockSpec((CB_TM, D_MODEL), lambda i: (i, 0)),
        out_shape=jax.ShapeDtypeStruct((T, D_MODEL), F32),
        scratch_shapes=[pltpu.VMEM((CB_TM, D_MODEL), F32)],
        compiler_params=_cparams(("parallel",)),
        name="combine",
    )(h, gates, g_final, ybuf, ybuf, ybuf, ybuf)


def _group_tables(counts, top_i, n_groups, T):
    A = T * TOP_K
    counts = counts.reshape(N_EXPERTS)
    groups_e = (counts + ROW_GROUP - 1) // ROW_GROUP
    gend = jnp.cumsum(groups_e)
    gstart = gend - groups_e
    total = gend[-1]
    gidx = jnp.arange(n_groups, dtype=jnp.int32)
    used = gidx < total
    last = jnp.maximum(total - 1, 0)
    gsafe = jnp.where(used, gidx, last)
    g_e = jnp.minimum(jnp.sum(gend[None, :] <= gsafe[:, None], axis=1), N_EXPERTS - 1).astype(jnp.int32)
    rows = jnp.clip(counts[g_e] - (gsafe - gstart[g_e]) * ROW_GROUP, 0, ROW_GROUP)
    g_rows = jnp.where(used, rows, 0).astype(jnp.int32)
    cstart = jnp.cumsum(counts) - counts
    g_off = (cstart[g_e] + (gsafe - gstart[g_e]) * ROW_GROUP).astype(jnp.int32)
    a = jnp.arange(A, dtype=jnp.int32)
    dense = lax.sort(top_i.reshape(A) * A + a) % A
    tok = dense // TOP_K
    lists = jnp.stack([tok * ROW_C, ((dense % TOP_K) * T + tok) * ROW_C])
    dense2 = jnp.concatenate([lists, jnp.zeros((2, IDX_LEN), jnp.int32)], axis=1)
    return g_e, g_rows, g_off, dense2


def kernel(x, positions, norm_mix_g, w_in, w_out, rel_bias, norm_ffn_g, router_w, router_b,
           w_gate, b_gate, w_up, b_up, w_down, b_down, norm_final_g):
    B, S, D = x.shape
    T = B * S
    depth = norm_mix_g.shape[0]
    assert depth == 1, "the combine kernel applies the final norm, so exactly one layer is supported"
    n_groups = (T * TOP_K) // ROW_GROUP + N_EXPERTS
    h = x.reshape(T, D)
    pos2 = positions.reshape(T, 1)
    for l in range(depth):
        proj = _in_proj(h, norm_mix_g[l].reshape(1, D), w_in[l].astype(BF16))
        table = _bias_table(rel_bias[l])
        ret = _retention(proj, pos2, B, S)
        att = _attention(proj, table, B, S)
        h, hn_rows, top_i, gates, counts = _out_router(
            h, ret, att, w_out[l].astype(BF16), norm_ffn_g[l].reshape(1, D),
            router_w[l], router_b[l].reshape(1, N_EXPERTS))
        g_e, g_rows, g_off, dense2 = _group_tables(counts, top_i, n_groups, T)
        ybuf = _experts(g_e, g_rows, g_off, dense2, hn_rows, w_gate[l], b_gate[l], w_up[l], b_up[l],
                        w_down[l], b_down[l], n_groups, T * TOP_K)
        h = _combine(h, gates, norm_final_g.reshape(1, D), ybuf)
    return h.reshape(B, S, D)
```

```python
import functools
import math

import jax
import jax.numpy as jnp
import numpy as np
from jax import lax
from jax.experimental import pallas as pl
from jax.experimental.pallas import tpu as pltpu

D_MODEL = 2048
CHUNK = 64
RET_WIDTH = 1024
ATT_WIDTH = 1024
RET_HEAD_DIM = 256
RET_HEADS = RET_WIDTH // RET_HEAD_DIM
ATT_HEAD_DIM = 128
ATT_HEADS = ATT_WIDTH // ATT_HEAD_DIM
LEFT_CHUNKS = 8
MAX_REL = 256
REL_SIZE = MAX_REL + CHUNK
ROPE_BASE = 10000.0
N_EXPERTS = 32
TOP_K = 4
D_FF = D_MODEL
SWIGLU_LIMIT = 7.0
SWIGLU_ALPHA = 1.702
NORM_EPS = 1e-6
IN_COLS = 4 * RET_WIDTH + 3 * ATT_WIDTH
NEG_INF = -1e30

BF16 = jnp.bfloat16
F32 = jnp.float32

VMEM_LIMIT_BYTES = 56 * 1024 * 1024

ATT_QBLK = 256
ATT_LEFT = LEFT_CHUNKS * CHUNK
ATT_WIN = ATT_LEFT + ATT_QBLK
ROLL_W = 1024

ROW_GROUP = 1152
SUB_ROWS = 128
SUB_SHIFT = SUB_ROWS.bit_length() - 1
assert 1 << SUB_SHIFT == SUB_ROWS
FF_TILE = 256
LANES = 128
ROW_C = D_MODEL // LANES
GATHER_ROWS = ROW_GROUP // (D_FF // FF_TILE)
IDX_ALIGN = 1024
IDX_ALIGN_SHIFT = IDX_ALIGN.bit_length() - 1
IDX_LEN = IDX_ALIGN * (-(-(IDX_ALIGN - 1 + ROW_GROUP) // IDX_ALIGN))
IDX_SLOTS = 3
BIG_SUBS = (ROW_GROUP // SUB_ROWS - 1, ROW_GROUP // SUB_ROWS)


def _cparams(sem):
    return pltpu.CompilerParams(dimension_semantics=sem, vmem_limit_bytes=VMEM_LIMIT_BYTES)


def _in_proj_kernel(x_ref, g_ref, w_ref, o_ref, hn_ref):
    @pl.when(pl.program_id(1) == 0)
    def _():
        x = x_ref[...]
        ms = jnp.mean(x * x, axis=-1, keepdims=True)
        hn_ref[...] = (x * lax.rsqrt(ms + NORM_EPS) * g_ref[...]).astype(BF16)

    o_ref[...] = jnp.dot(hn_ref[...], w_ref[...], preferred_element_type=F32).astype(o_ref.dtype)


def _in_proj(x2, g, w_bf):
    T = x2.shape[0]
    tm, tn = 1024, 1792
    return pl.pallas_call(
        _in_proj_kernel,
        grid=(T // tm, IN_COLS // tn),
        in_specs=[
            pl.BlockSpec((tm, D_MODEL), lambda i, j: (i, 0)),
            pl.BlockSpec((1, D_MODEL), lambda i, j: (0, 0)),
            pl.BlockSpec((D_MODEL, tn), lambda i, j: (0, j)),
        ],
        out_specs=pl.BlockSpec((tm, tn), lambda i, j: (i, j)),
        out_shape=jax.ShapeDtypeStruct((T, IN_COLS), BF16),
        scratch_shapes=[pltpu.VMEM((tm, D_MODEL), BF16)],
        compiler_params=_cparams(("parallel", "arbitrary")),
        name="in_proj",
    )(x2, g, w_bf)


def _bias_table_kernel(sel_ref, rb_ref, o_ref):
    rb = rb_ref[...]
    hi = rb.astype(BF16)
    r1 = rb - hi.astype(F32)
    mid = r1.astype(BF16)
    lo = (r1 - mid.astype(F32)).astype(BF16)
    sel = sel_ref[...]
    frow = (jnp.dot(hi, sel, preferred_element_type=F32)
            + jnp.dot(mid, sel, preferred_element_type=F32)
            + jnp.dot(lo, sel, preferred_element_type=F32))
    i = lax.broadcasted_iota(jnp.int32, (ATT_QBLK, ATT_WIN), 0)
    c = lax.broadcasted_iota(jnp.int32, (ATT_QBLK, ATT_WIN), 1)
    qc = (i + ATT_LEFT) // CHUNK
    kc = c // CHUNK
    valid = (kc <= qc) & (kc >= qc - LEFT_CHUNKS)
    for h in range(ATT_HEADS):
        rows = jnp.broadcast_to(frow[h:h + 1, :], (ATT_QBLK, ROLL_W))
        rolled = pltpu.roll(rows, 0, 1, stride=1, stride_axis=0)
        o_ref[h] = jnp.where(valid, rolled[:, :ATT_WIN], NEG_INF)


def _bias_table(rel_bias):
    u = np.arange(-(ATT_QBLK - 1), ATT_WIN)
    idx = np.clip(ATT_LEFT - u, -(CHUNK - 1), MAX_REL) + (CHUNK - 1)
    sel = np.zeros((REL_SIZE, ROLL_W), np.float32)
    sel[idx, u % ROLL_W] = 1.0
    sel = jnp.asarray(sel, BF16)
    return pl.pallas_call(
        _bias_table_kernel,
        out_shape=jax.ShapeDtypeStruct((ATT_HEADS, ATT_QBLK, ATT_WIN), F32),
        compiler_params=pltpu.CompilerParams(vmem_limit_bytes=VMEM_LIMIT_BYTES),
        name="bias_table",
    )(sel, rel_bias)


RET_RB = 256


def _retention_kernel(pos_ref, invf_ref, q_ref, k_ref, v_ref, g_ref,
                      dintra_ref, dq_ref, dk_ref, dchunk_ref, o_ref, state_ref):
    @pl.when(pl.program_id(1) == 0)
    def _():
        state_ref[...] = jnp.zeros_like(state_ref)

    half = RET_HEAD_DIM // 2
    ang = pos_ref[...].astype(F32) * invf_ref[...]
    cos = jnp.cos(ang)
    sin = jnp.sin(ang)
    kscale = RET_HEAD_DIM ** -0.5

    for h in range(RET_HEADS):
        c0 = h * RET_HEAD_DIM
        q = q_ref[:, c0:c0 + RET_HEAD_DIM].astype(F32)
        k = k_ref[:, c0:c0 + RET_HEAD_DIM].astype(F32)
        q1, q2 = q[:, :half], q[:, half:]
        k1, k2 = k[:, :half], k[:, half:]
        qr = jnp.concatenate([q1 * cos - q2 * sin, q1 * sin + q2 * cos], axis=-1)
        kr = jnp.concatenate([k1 * cos - k2 * sin, k1 * sin + k2 * cos], axis=-1) * kscale
        d_intra = dintra_ref[h]
        d_q = dq_ref[h]
        d_k = dk_ref[h]
        d_c = dchunk_ref[h]
        for c in range(RET_RB // CHUNK):
            r0 = c * CHUNK
            qc = qr[r0:r0 + CHUNK]
            kc = kr[r0:r0 + CHUNK]
            vc = v_ref[r0:r0 + CHUNK, c0:c0 + RET_HEAD_DIM]
            qb = qc.astype(BF16)
            s = lax.dot_general(qb, kc.astype(BF16), (((1,), (1,)), ((), ())),
                                preferred_element_type=F32) * d_intra
            o = jnp.dot(s.astype(BF16), vc, preferred_element_type=F32)
            st = state_ref[h]
            o = o + jnp.dot((qc * d_q).astype(BF16), st.astype(BF16), preferred_element_type=F32)
            kv = lax.dot_general((kc * d_k).astype(BF16), vc, (((0,), (0,)), ((), ())),
                                 preferred_element_type=F32)
            state_ref[h] = st * d_c + kv
            o = o * lax.rsqrt(jnp.mean(o * o, axis=-1, keepdims=True) + NORM_EPS)
            g = g_ref[r0:r0 + CHUNK, c0:c0 + RET_HEAD_DIM].astype(F32)
            o_ref[r0:r0 + CHUNK, c0:c0 + RET_HEAD_DIM] = (o * (g * jax.nn.sigmoid(g))).astype(o_ref.dtype)


def _retention(proj, pos2, B, S):
    T = B * S
    nb = S // RET_RB
    half = RET_HEAD_DIM // 2
    L = CHUNK
    inv_freq = (1.0 / (ROPE_BASE ** (jnp.arange(half, dtype=F32) / half))).reshape(1, half)
    log_gamma = jnp.log(1.0 - jnp.exp(jnp.linspace(math.log(1.0 / 32), math.log(1.0 / 512), RET_HEADS)))
    n = jnp.arange(L, dtype=F32)
    d_intra = jnp.exp(jnp.abs(n[:, None] - n[None, :])[None] * log_gamma[:, None, None]).astype(F32)
    d_q = jnp.exp((n[None, :] + 1.0) * log_gamma[:, None]).astype(F32)[:, :, None]
    d_k = jnp.exp((L - 1.0 - n)[None, :] * log_gamma[:, None]).astype(F32)[:, :, None]
    d_chunk = jnp.exp(L * log_gamma).astype(F32).reshape(RET_HEADS, 1, 1)

    def col(j):
        return pl.BlockSpec((RET_RB, RET_WIDTH), lambda b, c: (b * nb + c, j))

    def full(shape):
        return pl.BlockSpec(shape, lambda b, c: (0,) * len(shape))

    return pl.pallas_call(
        _retention_kernel,
        grid=(B, nb),
        in_specs=[
            pl.BlockSpec((RET_RB, 1), lambda b, c: (b * nb + c, 0)),
            full((1, half)),
            col(0), col(1), col(2), col(3),
            full((RET_HEADS, L, L)), full((RET_HEADS, L, 1)), full((RET_HEADS, L, 1)),
            full((RET_HEADS, 1, 1)),
        ],
        out_specs=pl.BlockSpec((RET_RB, RET_WIDTH), lambda b, c: (b * nb + c, 0)),
        out_shape=jax.ShapeDtypeStruct((T, RET_WIDTH), BF16),
        scratch_shapes=[pltpu.VMEM((RET_HEADS, RET_HEAD_DIM, RET_HEAD_DIM), F32)],
        compiler_params=_cparams(("parallel", "arbitrary")),
        name="retention",
    )(pos2, inv_freq, proj, proj, proj, proj, d_intra, d_q, d_k, d_chunk)


def _attention_kernel(q_ref, k_ref, v_ref, tab_ref, o_ref):
    qb = pl.program_id(1)
    scale = ATT_HEAD_DIM ** -0.5

    def run(k_start, n_keys, col0):
        for h in range(ATT_HEADS):
            c0 = h * ATT_HEAD_DIM
            q = q_ref[:, c0:c0 + ATT_HEAD_DIM]
            k = k_ref[pl.ds(k_start, n_keys), c0:c0 + ATT_HEAD_DIM]
            v = v_ref[pl.ds(k_start, n_keys), c0:c0 + ATT_HEAD_DIM]
            s = lax.dot_general(q, k, (((1,), (1,)), ((), ())), preferred_element_type=F32)
            s = s * scale + tab_ref[h, :, col0:col0 + n_keys]
            m = jnp.max(s, axis=-1, keepdims=True)
            e = jnp.exp(s - m)
            o = jnp.dot(e.astype(BF16), v, preferred_element_type=F32) / jnp.sum(e, axis=-1, keepdims=True)
            o_ref[:, c0:c0 + ATT_HEAD_DIM] = o.astype(o_ref.dtype)

    @pl.when(qb == 0)
    def _():
        run(0, ATT_QBLK, ATT_LEFT)

    @pl.when(qb == 1)
    def _():
        run(0, 2 * ATT_QBLK, ATT_QBLK)

    @pl.when(qb >= 2)
    def _():
        run(pl.multiple_of((qb - 2) * ATT_QBLK, ATT_QBLK), ATT_WIN, 0)


def _attention(proj, table, B, S):
    T = B * S
    nq = S // ATT_QBLK
    return pl.pallas_call(
        _attention_kernel,
        grid=(B, nq),
        in_specs=[
            pl.BlockSpec((ATT_QBLK, ATT_WIDTH), lambda b, q: (b * nq + q, 4)),
            pl.BlockSpec((S, ATT_WIDTH), lambda b, q: (b, 5)),
            pl.BlockSpec((S, ATT_WIDTH), lambda b, q: (b, 6)),
            pl.BlockSpec((ATT_HEADS, ATT_QBLK, ATT_WIN), lambda b, q: (0, 0, 0)),
        ],
        out_specs=pl.BlockSpec((ATT_QBLK, ATT_WIDTH), lambda b, q: (b * nq + q, 0)),
        out_shape=jax.ShapeDtypeStruct((T, ATT_WIDTH), BF16),
        compiler_params=_cparams(("parallel", "arbitrary")),
        name="attention",
    )(proj, proj, proj, table)


OR_TM = 512


def _out_router_kernel(x_ref, ret_ref, att_ref, wo_ref, g_ref, rw_ref, rb_ref,
                       h_ref, hn_ref, ti_ref, gt_ref, cnt_ref, carry_ref):
    @pl.when(pl.program_id(0) == 0)
    def _():
        carry_ref[...] = jnp.zeros_like(carry_ref)

    mix = (jnp.dot(ret_ref[...], wo_ref[:RET_WIDTH, :], preferred_element_type=F32)
           + jnp.dot(att_ref[...], wo_ref[RET_WIDTH:, :], preferred_element_type=F32))
    h = x_ref[...] + mix
    h_ref[...] = h
    hn = h * lax.rsqrt(jnp.mean(h * h, axis=-1, keepdims=True) + NORM_EPS) * g_ref[...]
    hn_bf = hn.astype(BF16)
    for c in range(ROW_C):
        hn_ref[pl.ds(c, OR_TM, stride=ROW_C), :] = hn[:, c * LANES:(c + 1) * LANES]

    hn_lo = (hn - hn_bf.astype(F32)).astype(BF16)
    rw = rw_ref[...]
    rw_hi = rw.astype(BF16)
    rw_lo = (rw - rw_hi.astype(F32)).astype(BF16)
    hi_both = jnp.dot(hn_bf, jnp.concatenate([rw_hi, rw_lo], axis=1), preferred_element_type=F32)
    logits = (hi_both[:, :N_EXPERTS] + hi_both[:, N_EXPERTS:]
              + jnp.dot(hn_lo, rw_hi, preferred_element_type=F32)) + rb_ref[...]

    lane = lax.broadcasted_iota(jnp.int32, (OR_TM, N_EXPERTS), 1)
    work = logits
    vals, idxs = [], []
    for _ in range(TOP_K):
        m = jnp.max(work, axis=-1, keepdims=True)
        idx = jnp.min(jnp.where(work == m, lane, N_EXPERTS), axis=-1, keepdims=True)
        vals.append(m)
        idxs.append(idx)
        work = jnp.where(lane == idx, -jnp.inf, work)
    es = [jnp.exp(v - vals[0]) for v in vals]
    den = es[0] + es[1] + es[2] + es[3]

    sel = jnp.zeros((OR_TM, N_EXPERTS), F32)
    for idx in idxs:
        sel = sel + (lane == idx).astype(F32)
    for kk in range(TOP_K):
        ti_ref[:, kk:kk + 1] = idxs[kk]
        gt_ref[:, kk:kk + 1] = es[kk] / den
    carry_ref[...] = carry_ref[...] + jnp.sum(sel, axis=0, keepdims=True)
    cnt_ref[...] = carry_ref[...].astype(jnp.int32)


def _out_router(x2, ret, att, wo_bf, g, rw, rb):
    T = x2.shape[0]
    tm = OR_TM
    row = lambda n: pl.BlockSpec((tm, n), lambda i: (i, 0))
    fix = lambda a, b: pl.BlockSpec((a, b), lambda i: (0, 0))
    return pl.pallas_call(
        _out_router_kernel,
        grid=(T // tm,),
        in_specs=[row(D_MODEL), row(RET_WIDTH), row(ATT_WIDTH), fix(D_MODEL, D_MODEL),
                  fix(1, D_MODEL), fix(D_MODEL, N_EXPERTS), fix(1, N_EXPERTS)],
        out_specs=[row(D_MODEL), pl.BlockSpec((tm * ROW_C, LANES), lambda i: (i, 0)),
                   row(TOP_K), row(TOP_K), fix(1, N_EXPERTS)],
        out_shape=[
            jax.ShapeDtypeStruct((T, D_MODEL), F32),
            jax.ShapeDtypeStruct((T * ROW_C, LANES), F32),
            jax.ShapeDtypeStruct((T, TOP_K), jnp.int32),
            jax.ShapeDtypeStruct((T, TOP_K), F32),
            jax.ShapeDtypeStruct((1, N_EXPERTS), jnp.int32),
        ],
        scratch_shapes=[pltpu.VMEM((1, N_EXPERTS), F32)],
        compiler_params=_cparams(("arbitrary",)),
        name="out_router",
    )(x2, ret, att, wo_bf, g, rw, rb)


def _experts_kernel(ge_ref, grows_ref, goff_ref, dense_hbm, hn_hbm,
                    wg_ref, bg_ref, wu_ref, bu_ref, wd_ref, bd_ref, y_hbm,
                    xg, xbf, acc, stage, idx_buf, sem_g, sem_s, sem_i):
    g = pl.program_id(0)
    f = pl.program_id(1)
    n_f = pl.num_programs(1)
    n_g = pl.num_programs(0)
    nrows = grows_ref[g]
    nsub = lax.shift_right_logical(nrows + (SUB_ROWS - 1), SUB_SHIFT)
    used = nrows > 0
    prev_used = (g > 0) & (grows_ref[jnp.maximum(g - 1, 0)] > 0)
    g_next = jnp.minimum(g + 1, n_g - 1)

    def idx_copies(gi):
        start = pl.multiple_of(lax.shift_right_logical(goff_ref[gi], IDX_ALIGN_SHIFT) * IDX_ALIGN, IDX_ALIGN)
        slot = lax.rem(gi, IDX_SLOTS)
        return [pltpu.make_async_copy(
                    dense_hbm.at[w, pl.ds(start, IDX_LEN)],
                    idx_buf.at[pl.ds(pl.multiple_of((slot * 2 + w) * IDX_LEN, IDX_LEN), IDX_LEN)],
                    sem_i.at[slot])
                for w in range(2)]

    def idx_base(gi, w):
        return (lax.rem(gi, IDX_SLOTS) * 2 + w) * IDX_LEN + (goff_ref[gi] & (IDX_ALIGN - 1))

    def gather_copy(batch, pos0, dst_row0, j):
        src0 = pl.multiple_of(idx_buf[pos0 + j], ROW_C)
        dst0 = pl.multiple_of(dst_row0 + j * ROW_C, ROW_C)
        return pltpu.make_async_copy(hn_hbm.at[pl.ds(src0, ROW_C)], xg.at[pl.ds(dst0, ROW_C)], sem_g.at[batch])

    def gather_wait_all():
        for b in range(ROW_GROUP // GATHER_ROWS):
            blk = xg.at[pl.ds(b * GATHER_ROWS * ROW_C, GATHER_ROWS * ROW_C)]
            pltpu.make_async_copy(blk, blk, sem_g.at[b]).wait()

    scatter_base = idx_base(g, 1)

    def scatter_copy(s, r0, j):
        src0 = pl.multiple_of(j * ROW_C, ROW_C)
        dst0 = pl.multiple_of(idx_buf[scatter_base + r0 + j], ROW_C)
        return pltpu.make_async_copy(stage.at[s, pl.ds(src0, ROW_C)], y_hbm.at[pl.ds(dst0, ROW_C)], sem_s)

    def scatter_wait_block(s):
        blk = stage.at[s]
        pltpu.make_async_copy(blk, blk, sem_s).wait()

    def stage_rows(s, val):
        for c in range(ROW_C):
            stage[s, pl.ds(c, SUB_ROWS, stride=ROW_C), :] = val[:, c * LANES:(c + 1) * LANES]

    def scatter_partial(s, r0):
        n_here = jnp.minimum(nrows - r0, SUB_ROWS)

        def start(j, _):
            scatter_copy(s, r0, j).start()
            return 0

        def wait(j, _):
            scatter_copy(s, r0, j).wait()
            return 0

        lax.fori_loop(0, n_here, start, 0)
        lax.fori_loop(0, n_here, wait, 0)

    @pl.when(used & (f == 0) & (g == 0))
    def _():
        for gi in range(2):
            for cp in idx_copies(gi):
                cp.start()
            for cp in idx_copies(gi):
                cp.wait()
        base0 = idx_base(0, 0)
        for b in range(ROW_GROUP // GATHER_ROWS):
            def start(j, _, b=b):
                gather_copy(b, base0 + b * GATHER_ROWS, b * GATHER_ROWS * ROW_C, j).start()
                return 0

            lax.fori_loop(0, GATHER_ROWS, start, 0)

    has_ahead = used & (g + 2 < n_g)

    @pl.when(has_ahead & (f == 0))
    def _():
        for cp in idx_copies(g + 2):
            cp.start()

    @pl.when(has_ahead & (f == n_f - 1))
    def _():
        for cp in idx_copies(g + 2):
            cp.wait()

    @pl.when((f == 0) & (used | prev_used))
    def _():
        gather_wait_all()

    @pl.when(used & (f == 0))
    def _():
        def unpack(s, _):
            r0 = pl.multiple_of(s * SUB_ROWS, SUB_ROWS)
            for c in range(ROW_C):
                v = xg[pl.ds(r0 * ROW_C + c, SUB_ROWS, stride=ROW_C), :]
                xbf[pl.ds(r0, SUB_ROWS), c * LANES:(c + 1) * LANES] = v.astype(BF16)
            acc[pl.ds(r0, SUB_ROWS), :] = jnp.broadcast_to(bd_ref[0], (SUB_ROWS, D_MODEL))
            return 0

        n_read = jnp.where(nsub == 1, 1, jnp.where(nsub <= BIG_SUBS[0], BIG_SUBS[0], BIG_SUBS[1]))
        lax.fori_loop(0, n_read, unpack, 0)

    def prefetch_next():
        pos0 = idx_base(g_next, 0) + f * GATHER_ROWS
        dst_row0 = f * (GATHER_ROWS * ROW_C)
        for j in range(GATHER_ROWS):
            gather_copy(f, pos0, dst_row0, j).start()

    def hidden(m):
        xb = xbf[0:m, :]
        gate = jnp.dot(xb, wg_ref[0].astype(BF16), preferred_element_type=F32) + bg_ref[0]
        gate = jnp.minimum(gate, SWIGLU_LIMIT)
        up = jnp.dot(xb, wu_ref[0].astype(BF16), preferred_element_type=F32) + bu_ref[0]
        up = jnp.clip(up, -SWIGLU_LIMIT, SWIGLU_LIMIT)
        return ((up + 1.0) * gate * jax.nn.sigmoid(SWIGLU_ALPHA * gate)).astype(BF16)

    def mlp_step(m):
        prefetch_next()
        acc[0:m, :] += jnp.dot(hidden(m), wd_ref[0].astype(BF16), preferred_element_type=F32)

    def mlp_last_big(n):
        prefetch_next()
        hid = hidden(n * SUB_ROWS)
        wd = wd_ref[0].astype(BF16)
        for s in range(n):
            r0 = s * SUB_ROWS
            out = acc[r0:r0 + SUB_ROWS, :] + jnp.dot(hid[r0:r0 + SUB_ROWS, :], wd, preferred_element_type=F32)
            stage_rows(s, out)
            if s < n - 1:
                for j in range(SUB_ROWS):
                    scatter_copy(s, r0, j).start()

    is_last = f == n_f - 1
    mid, top = BIG_SUBS
    big_last = is_last & ((nsub == mid) | (nsub == top))
    pl.when(nsub == 1)(functools.partial(mlp_step, SUB_ROWS))
    pl.when((nsub > 1) & (nsub <= mid) & jnp.logical_not(big_last))(functools.partial(mlp_step, mid * SUB_ROWS))
    pl.when((nsub > mid) & jnp.logical_not(big_last))(functools.partial(mlp_step, top * SUB_ROWS))
    for n in BIG_SUBS:
        pl.when(is_last & (nsub == n))(functools.partial(mlp_last_big, n))

    @pl.when(big_last)
    def _():
        scatter_partial(nsub - 1, pl.multiple_of((nsub - 1) * SUB_ROWS, SUB_ROWS))

        def wait(s, _):
            scatter_wait_block(s)
            return 0

        lax.fori_loop(0, nsub - 1, wait, 0)

    @pl.when(is_last & used & jnp.logical_not(big_last))
    def _():
        def emit(s, _):
            r0 = pl.multiple_of(s * SUB_ROWS, SUB_ROWS)
            stage_rows(0, acc[pl.ds(r0, SUB_ROWS), :])
            scatter_partial(0, r0)
            return 0

        lax.fori_loop(0, nsub, emit, 0)

    @pl.when(used & is_last & (g == n_g - 1))
    def _():
        gather_wait_all()


def _experts(g_e, g_rows, g_off, dense2, hn_rows, w_gate, b_gate, w_up, b_up, w_down, b_down, n_groups, n_yrows):
    nf = D_FF // FF_TILE
    bg3 = b_gate.reshape(N_EXPERTS, 1, D_FF)
    bu3 = b_up.reshape(N_EXPERTS, 1, D_FF)
    bd3 = b_down.reshape(N_EXPERTS, 1, D_MODEL)

    def ff(gi, fi, gn):
        return jnp.where(gn[gi] > 0, fi, nf - 1)

    grid_spec = pltpu.PrefetchScalarGridSpec(
        num_scalar_prefetch=3,
        grid=(n_groups, nf),
        in_specs=[
            pl.BlockSpec(memory_space=pl.ANY),
            pl.BlockSpec(memory_space=pl.ANY),
            pl.BlockSpec((1, D_MODEL, FF_TILE), lambda gi, fi, ge, gn, go: (ge[gi], 0, ff(gi, fi, gn))),
            pl.BlockSpec((1, 1, FF_TILE), lambda gi, fi, ge, gn, go: (ge[gi], 0, ff(gi, fi, gn))),
            pl.BlockSpec((1, D_MODEL, FF_TILE), lambda gi, fi, ge, gn, go: (ge[gi], 0, ff(gi, fi, gn))),
            pl.BlockSpec((1, 1, FF_TILE), lambda gi, fi, ge, gn, go: (ge[gi], 0, ff(gi, fi, gn))),
            pl.BlockSpec((1, FF_TILE, D_MODEL), lambda gi, fi, ge, gn, go: (ge[gi], ff(gi, fi, gn), 0)),
            pl.BlockSpec((1, 1, D_MODEL), lambda gi, fi, ge, gn, go: (ge[gi], 0, 0)),
        ],
        out_specs=pl.BlockSpec(memory_space=pl.ANY),
        scratch_shapes=[
            pltpu.VMEM((ROW_GROUP * ROW_C, LANES), F32),
            pltpu.VMEM((ROW_GROUP, D_MODEL), BF16),
            pltpu.VMEM((ROW_GROUP, D_MODEL), F32),
            pltpu.VMEM((ROW_GROUP // SUB_ROWS, SUB_ROWS * ROW_C, LANES), F32),
            pltpu.SMEM((IDX_SLOTS * 2 * IDX_LEN,), jnp.int32),
            pltpu.SemaphoreType.DMA((ROW_GROUP // GATHER_ROWS,)), pltpu.SemaphoreType.DMA(()),
            pltpu.SemaphoreType.DMA((IDX_SLOTS,)),
        ],
    )
    return pl.pallas_call(
        _experts_kernel,
        grid_spec=grid_spec,
        out_shape=jax.ShapeDtypeStruct((n_yrows * ROW_C, LANES), F32),
        compiler_params=_cparams(("arbitrary", "arbitrary")),
        name="experts",
    )(g_e, g_rows, g_off, dense2, hn_rows, w_gate, bg3, w_up, bu3, w_down, bd3)


CB_TM = 256


def _combine_kernel(h_ref, gt_ref, g_ref, y0_ref, y1_ref, y2_ref, y3_ref, o_ref, h2_ref):
    gt = gt_ref[...]
    ys = (y0_ref, y1_ref, y2_ref, y3_ref)
    for c in range(ROW_C):
        y = ys[0][pl.ds(c, CB_TM, stride=ROW_C), :] * gt[:, 0:1]
        for kk in range(1, TOP_K):
            y = y + ys[kk][pl.ds(c, CB_TM, stride=ROW_C), :] * gt[:, kk:kk + 1]
        h2_ref[:, c * LANES:(c + 1) * LANES] = h_ref[:, c * LANES:(c + 1) * LANES] + y
    h = h2_ref[...]
    o_ref[...] = h * lax.rsqrt(jnp.mean(h * h, axis=-1, keepdims=True) + NORM_EPS) * g_ref[...]


def _combine(h, gates, g_final, ybuf):
    T = h.shape[0]
    steps = T // CB_TM

    def yspec(kk):
        return pl.BlockSpec((CB_TM * ROW_C, LANES), lambda i: (kk * steps + i, 0))

    return pl.pallas_call(
        _combine_kernel,
        grid=(steps,),
        in_specs=[
            pl.BlockSpec((CB_TM, D_MODEL), lambda i: (i, 0)),
            pl.BlockSpec((CB_TM, TOP_K), lambda i: (i, 0)),
            pl.BlockSpec((1, D_MODEL), lambda i: (0, 0)),
            yspec(0), yspec(1), yspec(2), yspec(3),
        ],
        out_specs=pl.BlockSpec((CB_TM, D_MODEL), lambda i: (i, 0)),
        out_shape=jax.ShapeDtypeStruct((T, D_MODEL), F32),
        scratch_shapes=[pltpu.VMEM((CB_TM, D_MODEL), F32)],
        compiler_params=_cparams(("parallel",)),
        name="combine",
    )(h, gates, g_final, ybuf, ybuf, ybuf, ybuf)


def _group_tables(counts, top_i, n_groups, T):
    A = T * TOP_K
    counts = counts.reshape(N_EXPERTS)
    groups_e = (counts + ROW_GROUP - 1) // ROW_GROUP
    gend = jnp.cumsum(groups_e)
    gstart = gend - groups_e
    total = gend[-1]
    gidx = jnp.arange(n_groups, dtype=jnp.int32)
    used = gidx < total
    last = jnp.maximum(total - 1, 0)
    gsafe = jnp.where(used, gidx, last)
    g_e = jnp.minimum(jnp.sum(gend[None, :] <= gsafe[:, None], axis=1), N_EXPERTS - 1).astype(jnp.int32)
    rows = jnp.clip(counts[g_e] - (gsafe - gstart[g_e]) * ROW_GROUP, 0, ROW_GROUP)
    g_rows = jnp.where(used, rows, 0).astype(jnp.int32)
    cstart = jnp.cumsum(counts) - counts
    g_off = (cstart[g_e] + (gsafe - gstart[g_e]) * ROW_GROUP).astype(jnp.int32)
    a = jnp.arange(A, dtype=jnp.int32)
    dense = lax.sort(top_i.reshape(A) * A + a) % A
    tok = dense // TOP_K
    lists = jnp.stack([tok * ROW_C, ((dense % TOP_K) * T + tok) * ROW_C])
    dense2 = jnp.concatenate([lists, jnp.zeros((2, IDX_LEN), jnp.int32)], axis=1)
    return g_e, g_rows, g_off, dense2


def kernel(x, positions, norm_mix_g, w_in, w_out, rel_bias, norm_ffn_g, router_w, router_b,
           w_gate, b_gate, w_up, b_up, w_down, b_down, norm_final_g):
    B, S, D = x.shape
    T = B * S
    depth = norm_mix_g.shape[0]
    assert depth == 1, "the combine kernel applies the final norm, so exactly one layer is supported"
    n_groups = (T * TOP_K) // ROW_GROUP + N_EXPERTS
    h = x.reshape(T, D)
    pos2 = positions.reshape(T, 1)
    for l in range(depth):
        proj = _in_proj(h, norm_mix_g[l].reshape(1, D), w_in[l].astype(BF16))
        table = _bias_table(rel_bias[l])
        ret = _retention(proj, pos2, B, S)
        att = _attention(proj, table, B, S)
        h, hn_rows, top_i, gates, counts = _out_router(
            h, ret, att, w_out[l].astype(BF16), norm_ffn_g[l].reshape(1, D),
            router_w[l], router_b[l].reshape(1, N_EXPERTS))
        g_e, g_rows, g_off, dense2 = _group_tables(counts, top_i, n_groups, T)
        ybuf = _experts(g_e, g_rows, g_off, dense2, hn_rows, w_gate[l], b_gate[l], w_up[l], b_up[l],
                        w_down[l], b_down[l], n_groups, T * TOP_K)
        h = _combine(h, gates, norm_final_g.reshape(1, D), ybuf)
    return h.reshape(B, S, D)
```

```python
import functools
import math

import jax
import jax.numpy as jnp
import numpy as np
from jax import lax
from jax.experimental import pallas as pl
from jax.experimental.pallas import tpu as pltpu

D_MODEL = 2048
CHUNK = 64
RET_WIDTH = 1024
ATT_WIDTH = 1024
RET_HEAD_DIM = 256
RET_HEADS = RET_WIDTH // RET_HEAD_DIM
ATT_HEAD_DIM = 128
ATT_HEADS = ATT_WIDTH // ATT_HEAD_DIM
LEFT_CHUNKS = 8
MAX_REL = 256
REL_SIZE = MAX_REL + CHUNK
ROPE_BASE = 10000.0
N_EXPERTS = 32
TOP_K = 4
D_FF = D_MODEL
SWIGLU_LIMIT = 7.0
SWIGLU_ALPHA = 1.702
NORM_EPS = 1e-6
IN_COLS = 4 * RET_WIDTH + 3 * ATT_WIDTH
NEG_INF = -1e30

BF16 = jnp.bfloat16
F32 = jnp.float32

VMEM_LIMIT_BYTES = 60 * 1024 * 1024

ATT_QBLK = 256
ATT_LEFT = LEFT_CHUNKS * CHUNK
ATT_WIN = ATT_LEFT + ATT_QBLK
ROLL_W = 1024

ROW_GROUP = 1152
SUB_ROWS = 128
SUB_SHIFT = SUB_ROWS.bit_length() - 1
assert 1 << SUB_SHIFT == SUB_ROWS
FF_TILE = 256
LANES = 128
ROW_C = D_MODEL // LANES
GATHER_ROWS = ROW_GROUP // (D_FF // FF_TILE)
STAGE_PITCH = ROW_C + 8
IDX_ALIGN = 1024
IDX_ALIGN_SHIFT = IDX_ALIGN.bit_length() - 1
IDX_LEN = IDX_ALIGN * (-(-(IDX_ALIGN - 1 + ROW_GROUP) // IDX_ALIGN))
IDX_SLOTS = 3
BIG_SUBS = (ROW_GROUP // SUB_ROWS - 1, ROW_GROUP // SUB_ROWS)


def _cparams(sem):
    return pltpu.CompilerParams(dimension_semantics=sem, vmem_limit_bytes=VMEM_LIMIT_BYTES)


def _in_proj_kernel(x_ref, g_ref, w_ref, o_ref, hn_ref):
    @pl.when(pl.program_id(1) == 0)
    def _():
        x = x_ref[...]
        ms = jnp.mean(x * x, axis=-1, keepdims=True)
        hn_ref[...] = (x * lax.rsqrt(ms + NORM_EPS) * g_ref[...]).astype(BF16)

    o_ref[...] = jnp.dot(hn_ref[...], w_ref[...], preferred_element_type=F32).astype(o_ref.dtype)


def _in_proj(x2, g, w_bf):
    T = x2.shape[0]
    tm, tn = 1024, 1792
    return pl.pallas_call(
        _in_proj_kernel,
        grid=(T // tm, IN_COLS // tn),
        in_specs=[
            pl.BlockSpec((tm, D_MODEL), lambda i, j: (i, 0)),
            pl.BlockSpec((1, D_MODEL), lambda i, j: (0, 0)),
            pl.BlockSpec((D_MODEL, tn), lambda i, j: (0, j)),
        ],
        out_specs=pl.BlockSpec((tm, tn), lambda i, j: (i, j)),
        out_shape=jax.ShapeDtypeStruct((T, IN_COLS), BF16),
        scratch_shapes=[pltpu.VMEM((tm, D_MODEL), BF16)],
        compiler_params=_cparams(("parallel", "arbitrary")),
        name="in_proj",
    )(x2, g, w_bf)


def _bias_table_kernel(sel_ref, rb_ref, o_ref):
    rb = rb_ref[...]
    hi = rb.astype(BF16)
    r1 = rb - hi.astype(F32)
    mid = r1.astype(BF16)
    lo = (r1 - mid.astype(F32)).astype(BF16)
    sel = sel_ref[...]
    frow = (jnp.dot(hi, sel, preferred_element_type=F32)
            + jnp.dot(mid, sel, preferred_element_type=F32)
            + jnp.dot(lo, sel, preferred_element_type=F32))
    i = lax.broadcasted_iota(jnp.int32, (ATT_QBLK, ATT_WIN), 0)
    c = lax.broadcasted_iota(jnp.int32, (ATT_QBLK, ATT_WIN), 1)
    qc = (i + ATT_LEFT) // CHUNK
    kc = c // CHUNK
    valid = (kc <= qc) & (kc >= qc - LEFT_CHUNKS)
    for h in range(ATT_HEADS):
        rows = jnp.broadcast_to(frow[h:h + 1, :], (ATT_QBLK, ROLL_W))
        rolled = pltpu.roll(rows, 0, 1, stride=1, stride_axis=0)
        o_ref[h] = jnp.where(valid, rolled[:, :ATT_WIN], NEG_INF)


def _bias_table(rel_bias):
    u = np.arange(-(ATT_QBLK - 1), ATT_WIN)
    idx = np.clip(ATT_LEFT - u, -(CHUNK - 1), MAX_REL) + (CHUNK - 1)
    sel = np.zeros((REL_SIZE, ROLL_W), np.float32)
    sel[idx, u % ROLL_W] = 1.0
    sel = jnp.asarray(sel, BF16)
    return pl.pallas_call(
        _bias_table_kernel,
        out_shape=jax.ShapeDtypeStruct((ATT_HEADS, ATT_QBLK, ATT_WIN), F32),
        compiler_params=pltpu.CompilerParams(vmem_limit_bytes=VMEM_LIMIT_BYTES),
        name="bias_table",
    )(sel, rel_bias)


RET_RB = 256


def _retention_kernel(pos_ref, invf_ref, q_ref, k_ref, v_ref, g_ref,
                      dintra_ref, dq_ref, dk_ref, dchunk_ref, o_ref, state_ref):
    @pl.when(pl.program_id(1) == 0)
    def _():
        state_ref[...] = jnp.zeros_like(state_ref)

    half = RET_HEAD_DIM // 2
    ang = pos_ref[...].astype(F32) * invf_ref[...]
    cos = jnp.cos(ang)
    sin = jnp.sin(ang)
    kscale = RET_HEAD_DIM ** -0.5

    for h in range(RET_HEADS):
        c0 = h * RET_HEAD_DIM
        q = q_ref[:, c0:c0 + RET_HEAD_DIM].astype(F32)
        k = k_ref[:, c0:c0 + RET_HEAD_DIM].astype(F32)
        q1, q2 = q[:, :half], q[:, half:]
        k1, k2 = k[:, :half], k[:, half:]
        qr = jnp.concatenate([q1 * cos - q2 * sin, q1 * sin + q2 * cos], axis=-1)
        kr = jnp.concatenate([k1 * cos - k2 * sin, k1 * sin + k2 * cos], axis=-1) * kscale
        d_intra = dintra_ref[h]
        d_q = dq_ref[h]
        d_k = dk_ref[h]
        d_c = dchunk_ref[h]
        for c in range(RET_RB // CHUNK):
            r0 = c * CHUNK
            qc = qr[r0:r0 + CHUNK]
            kc = kr[r0:r0 + CHUNK]
            vc = v_ref[r0:r0 + CHUNK, c0:c0 + RET_HEAD_DIM]
            qb = qc.astype(BF16)
            s = lax.dot_general(qb, kc.astype(BF16), (((1,), (1,)), ((), ())),
                                preferred_element_type=F32) * d_intra
            o = jnp.dot(s.astype(BF16), vc, preferred_element_type=F32)
            st = state_ref[h]
            o = o + jnp.dot((qc * d_q).astype(BF16), st.astype(BF16), preferred_element_type=F32)
            kv = lax.dot_general((kc * d_k).astype(BF16), vc, (((0,), (0,)), ((), ())),
                                 preferred_element_type=F32)
            state_ref[h] = st * d_c + kv
            o = o * lax.rsqrt(jnp.mean(o * o, axis=-1, keepdims=True) + NORM_EPS)
            g = g_ref[r0:r0 + CHUNK, c0:c0 + RET_HEAD_DIM].astype(F32)
            o_ref[r0:r0 + CHUNK, c0:c0 + RET_HEAD_DIM] = (o * (g * jax.nn.sigmoid(g))).astype(o_ref.dtype)


def _retention(proj, pos2, B, S):
    T = B * S
    nb = S // RET_RB
    half = RET_HEAD_DIM // 2
    L = CHUNK
    inv_freq = (1.0 / (ROPE_BASE ** (jnp.arange(half, dtype=F32) / half))).reshape(1, half)
    log_gamma = jnp.log(1.0 - jnp.exp(jnp.linspace(math.log(1.0 / 32), math.log(1.0 / 512), RET_HEADS)))
    n = jnp.arange(L, dtype=F32)
    d_intra = jnp.exp(jnp.abs(n[:, None] - n[None, :])[None] * log_gamma[:, None, None]).astype(F32)
    d_q = jnp.exp((n[None, :] + 1.0) * log_gamma[:, None]).astype(F32)[:, :, None]
    d_k = jnp.exp((L - 1.0 - n)[None, :] * log_gamma[:, None]).astype(F32)[:, :, None]
    d_chunk = jnp.exp(L * log_gamma).astype(F32).reshape(RET_HEADS, 1, 1)

    def col(j):
        return pl.BlockSpec((RET_RB, RET_WIDTH), lambda b, c: (b * nb + c, j))

    def full(shape):
        return pl.BlockSpec(shape, lambda b, c: (0,) * len(shape))

    return pl.pallas_call(
        _retention_kernel,
        grid=(B, nb),
        in_specs=[
            pl.BlockSpec((RET_RB, 1), lambda b, c: (b * nb + c, 0)),
            full((1, half)),
            col(0), col(1), col(2), col(3),
            full((RET_HEADS, L, L)), full((RET_HEADS, L, 1)), full((RET_HEADS, L, 1)),
            full((RET_HEADS, 1, 1)),
        ],
        out_specs=pl.BlockSpec((RET_RB, RET_WIDTH), lambda b, c: (b * nb + c, 0)),
        out_shape=jax.ShapeDtypeStruct((T, RET_WIDTH), BF16),
        scratch_shapes=[pltpu.VMEM((RET_HEADS, RET_HEAD_DIM, RET_HEAD_DIM), F32)],
        compiler_params=_cparams(("parallel", "arbitrary")),
        name="retention",
    )(pos2, inv_freq, proj, proj, proj, proj, d_intra, d_q, d_k, d_chunk)


def _attention_kernel(q_ref, k_ref, v_ref, tab_ref, o_ref):
    qb = pl.program_id(1)
    scale = ATT_HEAD_DIM ** -0.5

    def run(k_start, n_keys, col0):
        for h in range(ATT_HEADS):
            c0 = h * ATT_HEAD_DIM
            q = q_ref[:, c0:c0 + ATT_HEAD_DIM]
            k = k_ref[pl.ds(k_start, n_keys), c0:c0 + ATT_HEAD_DIM]
            v = v_ref[pl.ds(k_start, n_keys), c0:c0 + ATT_HEAD_DIM]
            s = lax.dot_general(q, k, (((1,), (1,)), ((), ())), preferred_element_type=F32)
            s = s * scale + tab_ref[h, :, col0:col0 + n_keys]
            m = jnp.max(s, axis=-1, keepdims=True)
            e = jnp.exp(s - m)
            o = jnp.dot(e.astype(BF16), v, preferred_element_type=F32) / jnp.sum(e, axis=-1, keepdims=True)
            o_ref[:, c0:c0 + ATT_HEAD_DIM] = o.astype(o_ref.dtype)

    @pl.when(qb == 0)
    def _():
        run(0, ATT_QBLK, ATT_LEFT)

    @pl.when(qb == 1)
    def _():
        run(0, 2 * ATT_QBLK, ATT_QBLK)

    @pl.when(qb >= 2)
    def _():
        run(pl.multiple_of((qb - 2) * ATT_QBLK, ATT_QBLK), ATT_WIN, 0)


def _attention(proj, table, B, S):
    T = B * S
    nq = S // ATT_QBLK
    return pl.pallas_call(
        _attention_kernel,
        grid=(B, nq),
        in_specs=[
            pl.BlockSpec((ATT_QBLK, ATT_WIDTH), lambda b, q: (b * nq + q, 4)),
            pl.BlockSpec((S, ATT_WIDTH), lambda b, q: (b, 5)),
            pl.BlockSpec((S, ATT_WIDTH), lambda b, q: (b, 6)),
            pl.BlockSpec((ATT_HEADS, ATT_QBLK, ATT_WIN), lambda b, q: (0, 0, 0)),
        ],
        out_specs=pl.BlockSpec((ATT_QBLK, ATT_WIDTH), lambda b, q: (b * nq + q, 0)),
        out_shape=jax.ShapeDtypeStruct((T, ATT_WIDTH), BF16),
        compiler_params=_cparams(("parallel", "arbitrary")),
        name="attention",
    )(proj, proj, proj, table)


OR_TM = 512


def _out_router_kernel(x_ref, ret_ref, att_ref, wo_ref, g_ref, rw_ref, rb_ref,
                       h_ref, hn_ref, ti_ref, gt_ref, cnt_ref, carry_ref):
    @pl.when(pl.program_id(0) == 0)
    def _():
        carry_ref[...] = jnp.zeros_like(carry_ref)

    mix = (jnp.dot(ret_ref[...], wo_ref[:RET_WIDTH, :], preferred_element_type=F32)
           + jnp.dot(att_ref[...], wo_ref[RET_WIDTH:, :], preferred_element_type=F32))
    h = x_ref[...] + mix
    h_ref[...] = h
    hn = h * lax.rsqrt(jnp.mean(h * h, axis=-1, keepdims=True) + NORM_EPS) * g_ref[...]
    hn_bf = hn.astype(BF16)
    for c in range(ROW_C):
        hn_ref[pl.ds(c, OR_TM, stride=ROW_C), :] = hn[:, c * LANES:(c + 1) * LANES]

    hn_lo = (hn - hn_bf.astype(F32)).astype(BF16)
    rw = rw_ref[...]
    rw_hi = rw.astype(BF16)
    rw_lo = (rw - rw_hi.astype(F32)).astype(BF16)
    hi_both = jnp.dot(hn_bf, jnp.concatenate([rw_hi, rw_lo], axis=1), preferred_element_type=F32)
    logits = (hi_both[:, :N_EXPERTS] + hi_both[:, N_EXPERTS:]
              + jnp.dot(hn_lo, rw_hi, preferred_element_type=F32)) + rb_ref[...]

    lane = lax.broadcasted_iota(jnp.int32, (OR_TM, N_EXPERTS), 1)
    work = logits
    vals, idxs = [], []
    for _ in range(TOP_K):
        m = jnp.max(work, axis=-1, keepdims=True)
        idx = jnp.min(jnp.where(work == m, lane, N_EXPERTS), axis=-1, keepdims=True)
        vals.append(m)
        idxs.append(idx)
        work = jnp.where(lane == idx, -jnp.inf, work)
    es = [jnp.exp(v - vals[0]) for v in vals]
    den = es[0] + es[1] + es[2] + es[3]

    sel = jnp.zeros((OR_TM, N_EXPERTS), F32)
    for idx in idxs:
        sel = sel + (lane == idx).astype(F32)
    for kk in range(TOP_K):
        ti_ref[:, kk:kk + 1] = idxs[kk]
        gt_ref[:, kk:kk + 1] = es[kk] / den
    carry_ref[...] = carry_ref[...] + jnp.sum(sel, axis=0, keepdims=True)
    cnt_ref[...] = carry_ref[...].astype(jnp.int32)


def _out_router(x2, ret, att, wo_bf, g, rw, rb):
    T = x2.shape[0]
    tm = OR_TM
    row = lambda n: pl.BlockSpec((tm, n), lambda i: (i, 0))
    fix = lambda a, b: pl.BlockSpec((a, b), lambda i: (0, 0))
    return pl.pallas_call(
        _out_router_kernel,
        grid=(T // tm,),
        in_specs=[row(D_MODEL), row(RET_WIDTH), row(ATT_WIDTH), fix(D_MODEL, D_MODEL),
                  fix(1, D_MODEL), fix(D_MODEL, N_EXPERTS), fix(1, N_EXPERTS)],
        out_specs=[row(D_MODEL), pl.BlockSpec((tm * ROW_C, LANES), lambda i: (i, 0)),
                   row(TOP_K), row(TOP_K), fix(1, N_EXPERTS)],
        out_shape=[
            jax.ShapeDtypeStruct((T, D_MODEL), F32),
            jax.ShapeDtypeStruct((T * ROW_C, LANES), F32),
            jax.ShapeDtypeStruct((T, TOP_K), jnp.int32),
            jax.ShapeDtypeStruct((T, TOP_K), F32),
            jax.ShapeDtypeStruct((1, N_EXPERTS), jnp.int32),
        ],
        scratch_shapes=[pltpu.VMEM((1, N_EXPERTS), F32)],
        compiler_params=_cparams(("arbitrary",)),
        name="out_router",
    )(x2, ret, att, wo_bf, g, rw, rb)


def _experts_kernel(ge_ref, grows_ref, goff_ref, dense_hbm, hn_hbm,
                    wg_ref, bg_ref, wu_ref, bu_ref, wd_ref, bd_ref, y_hbm,
                    xg, xbf, acc, stage, idx_buf, sem_g, sem_s, sem_i):
    g = pl.program_id(0)
    f = pl.program_id(1)
    n_f = pl.num_programs(1)
    n_g = pl.num_programs(0)
    nrows = grows_ref[g]
    nsub = lax.shift_right_logical(nrows + (SUB_ROWS - 1), SUB_SHIFT)
    used = nrows > 0
    prev_used = (g > 0) & (grows_ref[jnp.maximum(g - 1, 0)] > 0)
    g_next = jnp.minimum(g + 1, n_g - 1)

    def idx_copies(gi):
        start = pl.multiple_of(lax.shift_right_logical(goff_ref[gi], IDX_ALIGN_SHIFT) * IDX_ALIGN, IDX_ALIGN)
        slot = lax.rem(gi, IDX_SLOTS)
        return [pltpu.make_async_copy(
                    dense_hbm.at[w, pl.ds(start, IDX_LEN)],
                    idx_buf.at[pl.ds(pl.multiple_of((slot * 2 + w) * IDX_LEN, IDX_LEN), IDX_LEN)],
                    sem_i.at[slot])
                for w in range(2)]

    def idx_base(gi, w):
        return (lax.rem(gi, IDX_SLOTS) * 2 + w) * IDX_LEN + (goff_ref[gi] & (IDX_ALIGN - 1))

    def gather_copy(batch, pos0, dst_row0, j):
        src0 = pl.multiple_of(idx_buf[pos0 + j], ROW_C)
        dst0 = pl.multiple_of(dst_row0 + j * STAGE_PITCH, 8)
        return pltpu.make_async_copy(hn_hbm.at[pl.ds(src0, ROW_C)], xg.at[pl.ds(dst0, ROW_C)], sem_g.at[batch])

    def gather_wait_all():
        for b in range(ROW_GROUP // GATHER_ROWS):
            blk = xg.at[pl.ds(0, GATHER_ROWS * ROW_C)]
            pltpu.make_async_copy(blk, blk, sem_g.at[b]).wait()

    scatter_base = idx_base(g, 1)

    def scatter_copy(s, r0, j):
        src0 = pl.multiple_of(j * STAGE_PITCH, 8)
        dst0 = pl.multiple_of(idx_buf[scatter_base + r0 + j], ROW_C)
        return pltpu.make_async_copy(stage.at[s, pl.ds(src0, ROW_C)], y_hbm.at[pl.ds(dst0, ROW_C)], sem_s)

    def scatter_wait_block(s):
        blk = stage.at[s, pl.ds(0, SUB_ROWS * ROW_C)]
        pltpu.make_async_copy(blk, blk, sem_s).wait()

    def stage_rows(s, val):
        for c in range(ROW_C):
            stage[s, pl.ds(c, SUB_ROWS, stride=STAGE_PITCH), :] = val[:, c * LANES:(c + 1) * LANES]

    def scatter_partial(s, r0):
        n_here = jnp.minimum(nrows - r0, SUB_ROWS)

        def start(j, _):
            scatter_copy(s, r0, j).start()
            return 0

        def wait(j, _):
            scatter_copy(s, r0, j).wait()
            return 0

        lax.fori_loop(0, n_here, start, 0)
        lax.fori_loop(0, n_here, wait, 0)

    @pl.when(used & (f == 0) & (g == 0))
    def _():
        for gi in range(2):
            for cp in idx_copies(gi):
                cp.start()
            for cp in idx_copies(gi):
                cp.wait()
        base0 = idx_base(0, 0)
        for b in range(ROW_GROUP // GATHER_ROWS):
            def start(j, _, b=b):
                gather_copy(b, base0 + b * GATHER_ROWS, b * GATHER_ROWS * STAGE_PITCH, j).start()
                return 0

            lax.fori_loop(0, GATHER_ROWS, start, 0)

    has_ahead = used & (g + 2 < n_g)

    @pl.when(has_ahead & (f == 0))
    def _():
        for cp in idx_copies(g + 2):
            cp.start()

    @pl.when(has_ahead & (f == n_f - 1))
    def _():
        for cp in idx_copies(g + 2):
            cp.wait()

    @pl.when((f == 0) & (used | prev_used))
    def _():
        gather_wait_all()

    @pl.when(used & (f == 0))
    def _():
        def unpack(s, _):
            r0 = pl.multiple_of(s * SUB_ROWS, SUB_ROWS)
            for c in range(ROW_C):
                v = xg[pl.ds(r0 * STAGE_PITCH + c, SUB_ROWS, stride=STAGE_PITCH), :]
                xbf[pl.ds(r0, SUB_ROWS), c * LANES:(c + 1) * LANES] = v.astype(BF16)
            acc[pl.ds(r0, SUB_ROWS), :] = jnp.broadcast_to(bd_ref[0], (SUB_ROWS, D_MODEL))
            return 0

        n_read = jnp.where(nsub == 1, 1, jnp.where(nsub <= BIG_SUBS[0], BIG_SUBS[0], BIG_SUBS[1]))
        lax.fori_loop(0, n_read, unpack, 0)

    def prefetch_next():
        pos0 = idx_base(g_next, 0) + f * GATHER_ROWS
        dst_row0 = f * (GATHER_ROWS * STAGE_PITCH)
        for j in range(GATHER_ROWS):
            gather_copy(f, pos0, dst_row0, j).start()

    def hidden(m):
        xb = xbf[0:m, :]
        gate = jnp.dot(xb, wg_ref[0].astype(BF16), preferred_element_type=F32) + bg_ref[0]
        gate = jnp.minimum(gate, SWIGLU_LIMIT)
        up = jnp.dot(xb, wu_ref[0].astype(BF16), preferred_element_type=F32) + bu_ref[0]
        up = jnp.clip(up, -SWIGLU_LIMIT, SWIGLU_LIMIT)
        return ((up + 1.0) * gate * jax.nn.sigmoid(SWIGLU_ALPHA * gate)).astype(BF16)

    def mlp_step(m):
        prefetch_next()
        acc[0:m, :] += jnp.dot(hidden(m), wd_ref[0].astype(BF16), preferred_element_type=F32)

    def mlp_last_big(n):
        prefetch_next()
        hid = hidden(n * SUB_ROWS)
        wd = wd_ref[0].astype(BF16)
        for s in range(n):
            r0 = s * SUB_ROWS
            out = acc[r0:r0 + SUB_ROWS, :] + jnp.dot(hid[r0:r0 + SUB_ROWS, :], wd, preferred_element_type=F32)
            stage_rows(s, out)
            if s < n - 1:
                for j in range(SUB_ROWS):
                    scatter_copy(s, r0, j).start()

    is_last = f == n_f - 1
    mid, top = BIG_SUBS
    big_last = is_last & ((nsub == mid) | (nsub == top))
    pl.when(nsub == 1)(functools.partial(mlp_step, SUB_ROWS))
    pl.when((nsub > 1) & (nsub <= mid) & jnp.logical_not(big_last))(functools.partial(mlp_step, mid * SUB_ROWS))
    pl.when((nsub > mid) & jnp.logical_not(big_last))(functools.partial(mlp_step, top * SUB_ROWS))
    for n in BIG_SUBS:
        pl.when(is_last & (nsub == n))(functools.partial(mlp_last_big, n))

    @pl.when(big_last)
    def _():
        scatter_partial(nsub - 1, pl.multiple_of((nsub - 1) * SUB_ROWS, SUB_ROWS))

        def wait(s, _):
            scatter_wait_block(s)
            return 0

        lax.fori_loop(0, nsub - 1, wait, 0)

    @pl.when(is_last & used & jnp.logical_not(big_last))
    def _():
        def emit(s, _):
            r0 = pl.multiple_of(s * SUB_ROWS, SUB_ROWS)
            stage_rows(0, acc[pl.ds(r0, SUB_ROWS), :])
            scatter_partial(0, r0)
            return 0

        lax.fori_loop(0, nsub, emit, 0)

    @pl.when(used & is_last & (g == n_g - 1))
    def _():
        gather_wait_all()


def _experts(g_e, g_rows, g_off, dense2, hn_rows, w_gate, b_gate, w_up, b_up, w_down, b_down, n_groups, n_yrows):
    nf = D_FF // FF_TILE
    bg3 = b_gate.reshape(N_EXPERTS, 1, D_FF)
    bu3 = b_up.reshape(N_EXPERTS, 1, D_FF)
    bd3 = b_down.reshape(N_EXPERTS, 1, D_MODEL)

    def ff(gi, fi, gn):
        return jnp.where(gn[gi] > 0, fi, nf - 1)

    grid_spec = pltpu.PrefetchScalarGridSpec(
        num_scalar_prefetch=3,
        grid=(n_groups, nf),
        in_specs=[
            pl.BlockSpec(memory_space=pl.ANY),
            pl.BlockSpec(memory_space=pl.ANY),
            pl.BlockSpec((1, D_MODEL, FF_TILE), lambda gi, fi, ge, gn, go: (ge[gi], 0, ff(gi, fi, gn))),
            pl.BlockSpec((1, 1, FF_TILE), lambda gi, fi, ge, gn, go: (ge[gi], 0, ff(gi, fi, gn))),
            pl.BlockSpec((1, D_MODEL, FF_TILE), lambda gi, fi, ge, gn, go: (ge[gi], 0, ff(gi, fi, gn))),
            pl.BlockSpec((1, 1, FF_TILE), lambda gi, fi, ge, gn, go: (ge[gi], 0, ff(gi, fi, gn))),
            pl.BlockSpec((1, FF_TILE, D_MODEL), lambda gi, fi, ge, gn, go: (ge[gi], ff(gi, fi, gn), 0)),
            pl.BlockSpec((1, 1, D_MODEL), lambda gi, fi, ge, gn, go: (ge[gi], 0, 0)),
        ],
        out_specs=pl.BlockSpec(memory_space=pl.ANY),
        scratch_shapes=[
            pltpu.VMEM((ROW_GROUP * STAGE_PITCH, LANES), F32),
            pltpu.VMEM((ROW_GROUP, D_MODEL), BF16),
            pltpu.VMEM((ROW_GROUP, D_MODEL), F32),
            pltpu.VMEM((ROW_GROUP // SUB_ROWS, SUB_ROWS * STAGE_PITCH, LANES), F32),
            pltpu.SMEM((IDX_SLOTS * 2 * IDX_LEN,), jnp.int32),
            pltpu.SemaphoreType.DMA((ROW_GROUP // GATHER_ROWS,)), pltpu.SemaphoreType.DMA(()),
            pltpu.SemaphoreType.DMA((IDX_SLOTS,)),
        ],
    )
    return pl.pallas_call(
        _experts_kernel,
        grid_spec=grid_spec,
        out_shape=jax.ShapeDtypeStruct((n_yrows * ROW_C, LANES), F32),
        compiler_params=_cparams(("arbitrary", "arbitrary")),
        name="experts",
    )(g_e, g_rows, g_off, dense2, hn_rows, w_gate, bg3, w_up, bu3, w_down, bd3)


CB_TM = 256


def _combine_kernel(h_ref, gt_ref, g_ref, y0_ref, y1_ref, y2_ref, y3_ref, o_ref, h2_ref):
    gt = gt_ref[...]
    ys = (y0_ref, y1_ref, y2_ref, y3_ref)
    for c in range(ROW_C):
        y = ys[0][pl.ds(c, CB_TM, stride=ROW_C), :] * gt[:, 0:1]
        for kk in range(1, TOP_K):
            y = y + ys[kk][pl.ds(c, CB_TM, stride=ROW_C), :] * gt[:, kk:kk + 1]
        h2_ref[:, c * LANES:(c + 1) * LANES] = h_ref[:, c * LANES:(c + 1) * LANES] + y
    h = h2_ref[...]
    o_ref[...] = h * lax.rsqrt(jnp.mean(h * h, axis=-1, keepdims=True) + NORM_EPS) * g_ref[...]


def _combine(h, gates, g_final, ybuf):
    T = h.shape[0]
    steps = T // CB_TM

    def yspec(kk):
        return pl.BlockSpec((CB_TM * ROW_C, LANES), lambda i: (kk * steps + i, 0))

    return pl.pallas_call(
        _combine_kernel,
        grid=(steps,),
        in_specs=[
            pl.BlockSpec((CB_TM, D_MODEL), lambda i: (i, 0)),
            pl.BlockSpec((CB_TM, TOP_K), lambda i: (i, 0)),
            pl.BlockSpec((1, D_MODEL), lambda i: (0, 0)),
            yspec(0), yspec(1), yspec(2), yspec(3),
        ],
        out_specs=pl.BlockSpec((CB_TM, D_MODEL), lambda i: (i, 0)),
        out_shape=jax.ShapeDtypeStruct((T, D_MODEL), F32),
        scratch_shapes=[pltpu.VMEM((CB_TM, D_MODEL), F32)],
        compiler_params=_cparams(("parallel",)),
        name="combine",
    )(h, gates, g_final, ybuf, ybuf, ybuf, ybuf)


def _group_tables(counts, top_i, n_groups, T):
    A = T * TOP_K
    counts = counts.reshape(N_EXPERTS)
    groups_e = (counts + ROW_GROUP - 1) // ROW_GROUP
    gend = jnp.cumsum(groups_e)
    gstart = gend - groups_e
    total = gend[-1]
    gidx = jnp.arange(n_groups, dtype=jnp.int32)
    used = gidx < total
    last = jnp.maximum(total - 1, 0)
    gsafe = jnp.where(used, gidx, last)
    g_e = jnp.minimum(jnp.sum(gend[None, :] <= gsafe[:, None], axis=1), N_EXPERTS - 1).astype(jnp.int32)
    rows = jnp.clip(counts[g_e] - (gsafe - gstart[g_e]) * ROW_GROUP, 0, ROW_GROUP)
    g_rows = jnp.where(used, rows, 0).astype(jnp.int32)
    cstart = jnp.cumsum(counts) - counts
    g_off = (cstart[g_e] + (gsafe - gstart[g_e]) * ROW_GROUP).astype(jnp.int32)
    a = jnp.arange(A, dtype=jnp.int32)
    dense = lax.sort(top_i.reshape(A) * A + a) % A
    tok = dense // TOP_K
    lists = jnp.stack([tok * ROW_C, ((dense % TOP_K) * T + tok) * ROW_C])
    dense2 = jnp.concatenate([lists, jnp.zeros((2, IDX_LEN), jnp.int32)], axis=1)
    return g_e, g_rows, g_off, dense2


def kernel(x, positions, norm_mix_g, w_in, w_out, rel_bias, norm_ffn_g, router_w, router_b,
           w_gate, b_gate, w_up, b_up, w_down, b_down, norm_final_g):
    B, S, D = x.shape
    T = B * S
    depth = norm_mix_g.shape[0]
    assert depth == 1, "the combine kernel applies the final norm, so exactly one layer is supported"
    n_groups = (T * TOP_K) // ROW_GROUP + N_EXPERTS
    h = x.reshape(T, D)
    pos2 = positions.reshape(T, 1)
    for l in range(depth):
        proj = _in_proj(h, norm_mix_g[l].reshape(1, D), w_in[l].astype(BF16))
        table = _bias_table(rel_bias[l])
        ret = _retention(proj, pos2, B, S)
        att = _attention(proj, table, B, S)
        h, hn_rows, top_i, gates, counts = _out_router(
            h, ret, att, w_out[l].astype(BF16), norm_ffn_g[l].reshape(1, D),
            router_w[l], router_b[l].reshape(1, N_EXPERTS))
        g_e, g_rows, g_off, dense2 = _group_tables(counts, top_i, n_groups, T)
        ybuf = _experts(g_e, g_rows, g_off, dense2, hn_rows, w_gate[l], b_gate[l], w_up[l], b_up[l],
                        w_down[l], b_down[l], n_groups, T * TOP_K)
        h = _combine(h, gates, norm_final_g.reshape(1, D), ybuf)
    return h.reshape(B, S, D)
```

```python
import functools
import math

import jax
import jax.numpy as jnp
import numpy as np
from jax import lax
from jax.experimental import pallas as pl
from jax.experimental.pallas import tpu as pltpu

D_MODEL = 2048
CHUNK = 64
RET_WIDTH = 1024
ATT_WIDTH = 1024
RET_HEAD_DIM = 256
RET_HEADS = RET_WIDTH // RET_HEAD_DIM
ATT_HEAD_DIM = 128
ATT_HEADS = ATT_WIDTH // ATT_HEAD_DIM
LEFT_CHUNKS = 8
MAX_REL = 256
REL_SIZE = MAX_REL + CHUNK
ROPE_BASE = 10000.0
N_EXPERTS = 32
TOP_K = 4
D_FF = D_MODEL
SWIGLU_LIMIT = 7.0
SWIGLU_ALPHA = 1.702
NORM_EPS = 1e-6
IN_COLS = 4 * RET_WIDTH + 3 * ATT_WIDTH
NEG_INF = -1e30

BF16 = jnp.bfloat16
F32 = jnp.float32

VMEM_LIMIT_BYTES = 60 * 1024 * 1024

ATT_QBLK = 256
ATT_LEFT = LEFT_CHUNKS * CHUNK
ATT_WIN = ATT_LEFT + ATT_QBLK
ROLL_W = 1024

ROW_GROUP = 1152
SUB_ROWS = 128
SUB_SHIFT = SUB_ROWS.bit_length() - 1
assert 1 << SUB_SHIFT == SUB_ROWS
FF_TILE = 256
LANES = 128
ROW_C = D_MODEL // LANES
GATHER_ROWS = ROW_GROUP // (D_FF // FF_TILE)
STAGE_PITCH = ROW_C + 8
IDX_ALIGN = 1024
IDX_ALIGN_SHIFT = IDX_ALIGN.bit_length() - 1
IDX_LEN = IDX_ALIGN * (-(-(IDX_ALIGN - 1 + ROW_GROUP) // IDX_ALIGN))
IDX_SLOTS = 3
BIG_SUBS = (ROW_GROUP // SUB_ROWS - 1, ROW_GROUP // SUB_ROWS)


def _cparams(sem):
    return pltpu.CompilerParams(dimension_semantics=sem, vmem_limit_bytes=VMEM_LIMIT_BYTES)


def _in_proj_kernel(x_ref, g_ref, w_ref, o_ref, hn_ref):
    @pl.when(pl.program_id(1) == 0)
    def _():
        x = x_ref[...]
        ms = jnp.mean(x * x, axis=-1, keepdims=True)
        hn_ref[...] = (x * lax.rsqrt(ms + NORM_EPS) * g_ref[...]).astype(BF16)

    o_ref[...] = jnp.dot(hn_ref[...], w_ref[...], preferred_element_type=F32).astype(o_ref.dtype)


def _in_proj(x2, g, w_bf):
    T = x2.shape[0]
    tm, tn = 1024, 1792
    return pl.pallas_call(
        _in_proj_kernel,
        grid=(T // tm, IN_COLS // tn),
        in_specs=[
            pl.BlockSpec((tm, D_MODEL), lambda i, j: (i, 0)),
            pl.BlockSpec((1, D_MODEL), lambda i, j: (0, 0)),
            pl.BlockSpec((D_MODEL, tn), lambda i, j: (0, j)),
        ],
        out_specs=pl.BlockSpec((tm, tn), lambda i, j: (i, j)),
        out_shape=jax.ShapeDtypeStruct((T, IN_COLS), BF16),
        scratch_shapes=[pltpu.VMEM((tm, D_MODEL), BF16)],
        compiler_params=_cparams(("parallel", "arbitrary")),
        name="in_proj",
    )(x2, g, w_bf)


def _bias_table_kernel(sel_ref, rb_ref, o_ref):
    rb = rb_ref[...]
    hi = rb.astype(BF16)
    r1 = rb - hi.astype(F32)
    mid = r1.astype(BF16)
    lo = (r1 - mid.astype(F32)).astype(BF16)
    sel = sel_ref[...]
    frow = (jnp.dot(hi, sel, preferred_element_type=F32)
            + jnp.dot(mid, sel, preferred_element_type=F32)
            + jnp.dot(lo, sel, preferred_element_type=F32))
    i = lax.broadcasted_iota(jnp.int32, (ATT_QBLK, ATT_WIN), 0)
    c = lax.broadcasted_iota(jnp.int32, (ATT_QBLK, ATT_WIN), 1)
    qc = (i + ATT_LEFT) // CHUNK
    kc = c // CHUNK
    valid = (kc <= qc) & (kc >= qc - LEFT_CHUNKS)
    for h in range(ATT_HEADS):
        rows = jnp.broadcast_to(frow[h:h + 1, :], (ATT_QBLK, ROLL_W))
        rolled = pltpu.roll(rows, 0, 1, stride=1, stride_axis=0)
        o_ref[h] = jnp.where(valid, rolled[:, :ATT_WIN], NEG_INF)


def _bias_table(rel_bias):
    u = np.arange(-(ATT_QBLK - 1), ATT_WIN)
    idx = np.clip(ATT_LEFT - u, -(CHUNK - 1), MAX_REL) + (CHUNK - 1)
    sel = np.zeros((REL_SIZE, ROLL_W), np.float32)
    sel[idx, u % ROLL_W] = 1.0
    sel = jnp.asarray(sel, BF16)
    return pl.pallas_call(
        _bias_table_kernel,
        out_shape=jax.ShapeDtypeStruct((ATT_HEADS, ATT_QBLK, ATT_WIN), F32),
        compiler_params=pltpu.CompilerParams(vmem_limit_bytes=VMEM_LIMIT_BYTES),
        name="bias_table",
    )(sel, rel_bias)


RET_RB = 256


def _retention_kernel(pos_ref, invf_ref, q_ref, k_ref, v_ref, g_ref,
                      dintra_ref, dq_ref, dk_ref, dchunk_ref, o_ref, state_ref):
    @pl.when(pl.program_id(1) == 0)
    def _():
        state_ref[...] = jnp.zeros_like(state_ref)

    half = RET_HEAD_DIM // 2
    ang = pos_ref[...].astype(F32) * invf_ref[...]
    cos = jnp.cos(ang)
    sin = jnp.sin(ang)
    kscale = RET_HEAD_DIM ** -0.5

    for h in range(RET_HEADS):
        c0 = h * RET_HEAD_DIM
        q = q_ref[:, c0:c0 + RET_HEAD_DIM].astype(F32)
        k = k_ref[:, c0:c0 + RET_HEAD_DIM].astype(F32)
        q1, q2 = q[:, :half], q[:, half:]
        k1, k2 = k[:, :half], k[:, half:]
        qr = jnp.concatenate([q1 * cos - q2 * sin, q1 * sin + q2 * cos], axis=-1)
        kr = jnp.concatenate([k1 * cos - k2 * sin, k1 * sin + k2 * cos], axis=-1) * kscale
        d_intra = dintra_ref[h]
        d_q = dq_ref[h]
        d_k = dk_ref[h]
        d_c = dchunk_ref[h]
        for c in range(RET_RB // CHUNK):
            r0 = c * CHUNK
            qc = qr[r0:r0 + CHUNK]
            kc = kr[r0:r0 + CHUNK]
            vc = v_ref[r0:r0 + CHUNK, c0:c0 + RET_HEAD_DIM]
            qb = qc.astype(BF16)
            s = lax.dot_general(qb, kc.astype(BF16), (((1,), (1,)), ((), ())),
                                preferred_element_type=F32) * d_intra
            o = jnp.dot(s.astype(BF16), vc, preferred_element_type=F32)
            st = state_ref[h]
            o = o + jnp.dot((qc * d_q).astype(BF16), st.astype(BF16), preferred_element_type=F32)
            kv = lax.dot_general((kc * d_k).astype(BF16), vc, (((0,), (0,)), ((), ())),
                                 preferred_element_type=F32)
            state_ref[h] = st * d_c + kv
            o = o * lax.rsqrt(jnp.mean(o * o, axis=-1, keepdims=True) + NORM_EPS)
            g = g_ref[r0:r0 + CHUNK, c0:c0 + RET_HEAD_DIM].astype(F32)
            o_ref[r0:r0 + CHUNK, c0:c0 + RET_HEAD_DIM] = (o * (g * jax.nn.sigmoid(g))).astype(o_ref.dtype)


def _retention(proj, pos2, B, S):
    T = B * S
    nb = S // RET_RB
    half = RET_HEAD_DIM // 2
    L = CHUNK
    inv_freq = (1.0 / (ROPE_BASE ** (jnp.arange(half, dtype=F32) / half))).reshape(1, half)
    log_gamma = jnp.log(1.0 - jnp.exp(jnp.linspace(math.log(1.0 / 32), math.log(1.0 / 512), RET_HEADS)))
    n = jnp.arange(L, dtype=F32)
    d_intra = jnp.exp(jnp.abs(n[:, None] - n[None, :])[None] * log_gamma[:, None, None]).astype(F32)
    d_q = jnp.exp((n[None, :] + 1.0) * log_gamma[:, None]).astype(F32)[:, :, None]
    d_k = jnp.exp((L - 1.0 - n)[None, :] * log_gamma[:, None]).astype(F32)[:, :, None]
    d_chunk = jnp.exp(L * log_gamma).astype(F32).reshape(RET_HEADS, 1, 1)

    def col(j):
        return pl.BlockSpec((RET_RB, RET_WIDTH), lambda b, c: (b * nb + c, j))

    def full(shape):
        return pl.BlockSpec(shape, lambda b, c: (0,) * len(shape))

    return pl.pallas_call(
        _retention_kernel,
        grid=(B, nb),
        in_specs=[
            pl.BlockSpec((RET_RB, 1), lambda b, c: (b * nb + c, 0)),
            full((1, half)),
            col(0), col(1), col(2), col(3),
            full((RET_HEADS, L, L)), full((RET_HEADS, L, 1)), full((RET_HEADS, L, 1)),
            full((RET_HEADS, 1, 1)),
        ],
        out_specs=pl.BlockSpec((RET_RB, RET_WIDTH), lambda b, c: (b * nb + c, 0)),
        out_shape=jax.ShapeDtypeStruct((T, RET_WIDTH), BF16),
        scratch_shapes=[pltpu.VMEM((RET_HEADS, RET_HEAD_DIM, RET_HEAD_DIM), F32)],
        compiler_params=_cparams(("parallel", "arbitrary")),
        name="retention",
    )(pos2, inv_freq, proj, proj, proj, proj, d_intra, d_q, d_k, d_chunk)


def _attention_kernel(q_ref, k_ref, v_ref, tab_ref, o_ref):
    qb = pl.program_id(1)
    scale = ATT_HEAD_DIM ** -0.5

    def run(k_start, n_keys, col0):
        for h in range(ATT_HEADS):
            c0 = h * ATT_HEAD_DIM
            q = q_ref[:, c0:c0 + ATT_HEAD_DIM]
            k = k_ref[pl.ds(k_start, n_keys), c0:c0 + ATT_HEAD_DIM]
            v = v_ref[pl.ds(k_start, n_keys), c0:c0 + ATT_HEAD_DIM]
            s = lax.dot_general(q, k, (((1,), (1,)), ((), ())), preferred_element_type=F32)
            s = s * scale + tab_ref[h, :, col0:col0 + n_keys]
            m = jnp.max(s, axis=-1, keepdims=True)
            e = jnp.exp(s - m)
            o = jnp.dot(e.astype(BF16), v, preferred_element_type=F32) / jnp.sum(e, axis=-1, keepdims=True)
            o_ref[:, c0:c0 + ATT_HEAD_DIM] = o.astype(o_ref.dtype)

    @pl.when(qb == 0)
    def _():
        run(0, ATT_QBLK, ATT_LEFT)

    @pl.when(qb == 1)
    def _():
        run(0, 2 * ATT_QBLK, ATT_QBLK)

    @pl.when(qb >= 2)
    def _():
        run(pl.multiple_of((qb - 2) * ATT_QBLK, ATT_QBLK), ATT_WIN, 0)


def _attention(proj, table, B, S):
    T = B * S
    nq = S // ATT_QBLK
    return pl.pallas_call(
        _attention_kernel,
        grid=(B, nq),
        in_specs=[
            pl.BlockSpec((ATT_QBLK, ATT_WIDTH), lambda b, q: (b * nq + q, 4)),
            pl.BlockSpec((S, ATT_WIDTH), lambda b, q: (b, 5)),
            pl.BlockSpec((S, ATT_WIDTH), lambda b, q: (b, 6)),
            pl.BlockSpec((ATT_HEADS, ATT_QBLK, ATT_WIN), lambda b, q: (0, 0, 0)),
        ],
        out_specs=pl.BlockSpec((ATT_QBLK, ATT_WIDTH), lambda b, q: (b * nq + q, 0)),
        out_shape=jax.ShapeDtypeStruct((T, ATT_WIDTH), BF16),
        compiler_params=_cparams(("parallel", "arbitrary")),
        name="attention",
    )(proj, proj, proj, table)


OR_TM = 512


def _out_router_kernel(x_ref, ret_ref, att_ref, wo_ref, g_ref, rw_ref, rb_ref,
                       h_ref, hn_ref, ti_ref, gt_ref, cnt_ref, carry_ref):
    @pl.when(pl.program_id(0) == 0)
    def _():
        carry_ref[...] = jnp.zeros_like(carry_ref)

    mix = (jnp.dot(ret_ref[...], wo_ref[:RET_WIDTH, :], preferred_element_type=F32)
           + jnp.dot(att_ref[...], wo_ref[RET_WIDTH:, :], preferred_element_type=F32))
    h = x_ref[...] + mix
    h_ref[...] = h
    hn = h * lax.rsqrt(jnp.mean(h * h, axis=-1, keepdims=True) + NORM_EPS) * g_ref[...]
    hn_bf = hn.astype(BF16)
    for c in range(ROW_C):
        hn_ref[pl.ds(c, OR_TM, stride=ROW_C), :] = hn[:, c * LANES:(c + 1) * LANES]

    hn_lo = (hn - hn_bf.astype(F32)).astype(BF16)
    rw = rw_ref[...]
    rw_hi = rw.astype(BF16)
    rw_lo = (rw - rw_hi.astype(F32)).astype(BF16)
    hi_both = jnp.dot(hn_bf, jnp.concatenate([rw_hi, rw_lo], axis=1), preferred_element_type=F32)
    logits = (hi_both[:, :N_EXPERTS] + hi_both[:, N_EXPERTS:]
              + jnp.dot(hn_lo, rw_hi, preferred_element_type=F32)) + rb_ref[...]

    lane = lax.broadcasted_iota(jnp.int32, (OR_TM, N_EXPERTS), 1)
    work = logits
    vals, idxs = [], []
    for _ in range(TOP_K):
        m = jnp.max(work, axis=-1, keepdims=True)
        idx = jnp.min(jnp.where(work == m, lane, N_EXPERTS), axis=-1, keepdims=True)
        vals.append(m)
        idxs.append(idx)
        work = jnp.where(lane == idx, -jnp.inf, work)
    es = [jnp.exp(v - vals[0]) for v in vals]
    den = es[0] + es[1] + es[2] + es[3]

    sel = jnp.zeros((OR_TM, N_EXPERTS), F32)
    for idx in idxs:
        sel = sel + (lane == idx).astype(F32)
    for kk in range(TOP_K):
        ti_ref[:, kk:kk + 1] = idxs[kk]
        gt_ref[:, kk:kk + 1] = es[kk] / den
    carry_ref[...] = carry_ref[...] + jnp.sum(sel, axis=0, keepdims=True)
    cnt_ref[...] = carry_ref[...].astype(jnp.int32)


def _out_router(x2, ret, att, wo_bf, g, rw, rb):
    T = x2.shape[0]
    tm = OR_TM
    row = lambda n: pl.BlockSpec((tm, n), lambda i: (i, 0))
    fix = lambda a, b: pl.BlockSpec((a, b), lambda i: (0, 0))
    return pl.pallas_call(
        _out_router_kernel,
        grid=(T // tm,),
        in_specs=[row(D_MODEL), row(RET_WIDTH), row(ATT_WIDTH), fix(D_MODEL, D_MODEL),
                  fix(1, D_MODEL), fix(D_MODEL, N_EXPERTS), fix(1, N_EXPERTS)],
        out_specs=[row(D_MODEL), pl.BlockSpec((tm * ROW_C, LANES), lambda i: (i, 0)),
                   row(TOP_K), row(TOP_K), fix(1, N_EXPERTS)],
        out_shape=[
            jax.ShapeDtypeStruct((T, D_MODEL), F32),
            jax.ShapeDtypeStruct((T * ROW_C, LANES), F32),
            jax.ShapeDtypeStruct((T, TOP_K), jnp.int32),
            jax.ShapeDtypeStruct((T, TOP_K), F32),
            jax.ShapeDtypeStruct((1, N_EXPERTS), jnp.int32),
        ],
        scratch_shapes=[pltpu.VMEM((1, N_EXPERTS), F32)],
        compiler_params=_cparams(("arbitrary",)),
        name="out_router",
    )(x2, ret, att, wo_bf, g, rw, rb)


def _experts_kernel(ge_ref, grows_ref, goff_ref, dense_hbm, hn_hbm,
                    wg_ref, bg_ref, wu_ref, bu_ref, wd_ref, bd_ref, y_hbm,
                    xg, xbf, acc, stage, idx_buf, sem_g, sem_s, sem_i):
    g = pl.program_id(0)
    f = pl.program_id(1)
    n_f = pl.num_programs(1)
    n_g = pl.num_programs(0)
    nrows = grows_ref[g]
    nsub = lax.shift_right_logical(nrows + (SUB_ROWS - 1), SUB_SHIFT)
    used = nrows > 0
    prev_used = (g > 0) & (grows_ref[jnp.maximum(g - 1, 0)] > 0)
    g_next = jnp.minimum(g + 1, n_g - 1)

    def idx_copies(gi):
        start = pl.multiple_of(lax.shift_right_logical(goff_ref[gi], IDX_ALIGN_SHIFT) * IDX_ALIGN, IDX_ALIGN)
        slot = lax.rem(gi, IDX_SLOTS)
        return [pltpu.make_async_copy(
                    dense_hbm.at[w, pl.ds(start, IDX_LEN)],
                    idx_buf.at[pl.ds(pl.multiple_of((slot * 2 + w) * IDX_LEN, IDX_LEN), IDX_LEN)],
                    sem_i.at[slot])
                for w in range(2)]

    def idx_base(gi, w):
        return (lax.rem(gi, IDX_SLOTS) * 2 + w) * IDX_LEN + (goff_ref[gi] & (IDX_ALIGN - 1))

    def gather_copy(batch, pos0, dst_row0, j):
        src0 = pl.multiple_of(idx_buf[pos0 + j], ROW_C)
        dst0 = pl.multiple_of(dst_row0 + j * STAGE_PITCH, 8)
        return pltpu.make_async_copy(hn_hbm.at[pl.ds(src0, ROW_C)], xg.at[pl.ds(dst0, ROW_C)], sem_g.at[batch])

    def gather_wait_all():
        for b in range(ROW_GROUP // GATHER_ROWS):
            blk = xg.at[pl.ds(0, GATHER_ROWS * ROW_C)]
            pltpu.make_async_copy(blk, blk, sem_g.at[b]).wait()

    scatter_base = idx_base(g, 1)

    def scatter_copy(s, r0, j):
        src0 = pl.multiple_of(j * STAGE_PITCH, 8)
        dst0 = pl.multiple_of(idx_buf[scatter_base + r0 + j], ROW_C)
        return pltpu.make_async_copy(stage.at[s, pl.ds(src0, ROW_C)], y_hbm.at[pl.ds(dst0, ROW_C)], sem_s)

    def scatter_wait_block(s):
        blk = stage.at[s, pl.ds(0, SUB_ROWS * ROW_C)]
        pltpu.make_async_copy(blk, blk, sem_s).wait()

    def stage_rows(s, val):
        for c in range(ROW_C):
            stage[s, pl.ds(c, val.shape[0], stride=STAGE_PITCH), :] = val[:, c * LANES:(c + 1) * LANES]

    def scatter_partial(s, r0):
        n_here = jnp.minimum(nrows - r0, SUB_ROWS)

        def start(j, _):
            scatter_copy(s, r0, j).start()
            return 0

        def wait(j, _):
            scatter_copy(s, r0, j).wait()
            return 0

        lax.fori_loop(0, n_here, start, 0)
        lax.fori_loop(0, n_here, wait, 0)

    @pl.when(used & (f == 0) & (g == 0))
    def _():
        for gi in range(2):
            for cp in idx_copies(gi):
                cp.start()
            for cp in idx_copies(gi):
                cp.wait()
        base0 = idx_base(0, 0)
        for b in range(ROW_GROUP // GATHER_ROWS):
            def start(j, _, b=b):
                gather_copy(b, base0 + b * GATHER_ROWS, b * GATHER_ROWS * STAGE_PITCH, j).start()
                return 0

            lax.fori_loop(0, GATHER_ROWS, start, 0)

    has_ahead = used & (g + 2 < n_g)

    @pl.when(has_ahead & (f == 0))
    def _():
        for cp in idx_copies(g + 2):
            cp.start()

    @pl.when(has_ahead & (f == n_f - 1))
    def _():
        for cp in idx_copies(g + 2):
            cp.wait()

    @pl.when((f == 0) & (used | prev_used))
    def _():
        gather_wait_all()

    @pl.when(used & (f == 0))
    def _():
        def unpack(s, _):
            r0 = pl.multiple_of(s * SUB_ROWS, SUB_ROWS)
            for c in range(ROW_C):
                v = xg[pl.ds(r0 * STAGE_PITCH + c, SUB_ROWS, stride=STAGE_PITCH), :]
                xbf[pl.ds(r0, SUB_ROWS), c * LANES:(c + 1) * LANES] = v.astype(BF16)
            acc[pl.ds(r0, SUB_ROWS), :] = jnp.broadcast_to(bd_ref[0], (SUB_ROWS, D_MODEL))
            return 0

        n_read = jnp.where(nsub == 1, 1, jnp.where(nsub <= BIG_SUBS[0], BIG_SUBS[0], BIG_SUBS[1]))
        lax.fori_loop(0, n_read, unpack, 0)

    def prefetch_next():
        pos0 = idx_base(g_next, 0) + f * GATHER_ROWS
        dst_row0 = f * (GATHER_ROWS * STAGE_PITCH)
        for j in range(GATHER_ROWS):
            gather_copy(f, pos0, dst_row0, j).start()

    def hidden(m):
        xb = xbf[0:m, :]
        gate = jnp.dot(xb, wg_ref[0].astype(BF16), preferred_element_type=F32) + bg_ref[0]
        gate = jnp.minimum(gate, SWIGLU_LIMIT)
        up = jnp.dot(xb, wu_ref[0].astype(BF16), preferred_element_type=F32) + bu_ref[0]
        up = jnp.clip(up, -SWIGLU_LIMIT, SWIGLU_LIMIT)
        return ((up + 1.0) * gate * jax.nn.sigmoid(SWIGLU_ALPHA * gate)).astype(BF16)

    def mlp_step(m):
        prefetch_next()
        acc[0:m, :] += jnp.dot(hidden(m), wd_ref[0].astype(BF16), preferred_element_type=F32)

    def mlp_last_big(n, m):
        prefetch_next()
        hid = hidden(m)
        wd = wd_ref[0].astype(BF16)
        for s in range(n):
            r0 = s * SUB_ROWS
            r1 = min(r0 + SUB_ROWS, m)
            out = acc[r0:r1, :] + jnp.dot(hid[r0:r1, :], wd, preferred_element_type=F32)
            stage_rows(s, out)
            if s < n - 1:
                for j in range(SUB_ROWS):
                    scatter_copy(s, r0, j).start()

    is_last = f == n_f - 1
    mid, top = BIG_SUBS
    m_mid, m_top, m_short = mid * SUB_ROWS, top * SUB_ROWS, top * SUB_ROWS - SUB_ROWS // 2
    short = nrows <= m_short
    big_last = is_last & ((nsub == mid) | (nsub == top))
    pl.when(nsub == 1)(functools.partial(mlp_step, SUB_ROWS))
    pl.when((nsub > 1) & (nsub <= mid) & jnp.logical_not(big_last))(functools.partial(mlp_step, m_mid))
    pl.when((nsub > mid) & short & jnp.logical_not(big_last))(functools.partial(mlp_step, m_short))
    pl.when((nsub > mid) & jnp.logical_not(short) & jnp.logical_not(big_last))(functools.partial(mlp_step, m_top))
    pl.when(is_last & (nsub == mid))(functools.partial(mlp_last_big, mid, m_mid))
    pl.when(is_last & (nsub == top) & short)(functools.partial(mlp_last_big, top, m_short))
    pl.when(is_last & (nsub == top) & jnp.logical_not(short))(functools.partial(mlp_last_big, top, m_top))

    @pl.when(big_last)
    def _():
        scatter_partial(nsub - 1, pl.multiple_of((nsub - 1) * SUB_ROWS, SUB_ROWS))

        def wait(s, _):
            scatter_wait_block(s)
            return 0

        lax.fori_loop(0, nsub - 1, wait, 0)

    @pl.when(is_last & used & jnp.logical_not(big_last))
    def _():
        def emit(s, _):
            r0 = pl.multiple_of(s * SUB_ROWS, SUB_ROWS)
            stage_rows(0, acc[pl.ds(r0, SUB_ROWS), :])
            scatter_partial(0, r0)
            return 0

        lax.fori_loop(0, nsub, emit, 0)

    @pl.when(used & is_last & (g == n_g - 1))
    def _():
        gather_wait_all()


def _experts(g_e, g_rows, g_off, dense2, hn_rows, w_gate, b_gate, w_up, b_up, w_down, b_down, n_groups, n_yrows):
    nf = D_FF // FF_TILE
    bg3 = b_gate.reshape(N_EXPERTS, 1, D_FF)
    bu3 = b_up.reshape(N_EXPERTS, 1, D_FF)
    bd3 = b_down.reshape(N_EXPERTS, 1, D_MODEL)

    def ff(gi, fi, gn):
        return jnp.where(gn[gi] > 0, fi, nf - 1)

    grid_spec = pltpu.PrefetchScalarGridSpec(
        num_scalar_prefetch=3,
        grid=(n_groups, nf),
        in_specs=[
            pl.BlockSpec(memory_space=pl.ANY),
            pl.BlockSpec(memory_space=pl.ANY),
            pl.BlockSpec((1, D_MODEL, FF_TILE), lambda gi, fi, ge, gn, go: (ge[gi], 0, ff(gi, fi, gn))),
            pl.BlockSpec((1, 1, FF_TILE), lambda gi, fi, ge, gn, go: (ge[gi], 0, ff(gi, fi, gn))),
            pl.BlockSpec((1, D_MODEL, FF_TILE), lambda gi, fi, ge, gn, go: (ge[gi], 0, ff(gi, fi, gn))),
            pl.BlockSpec((1, 1, FF_TILE), lambda gi, fi, ge, gn, go: (ge[gi], 0, ff(gi, fi, gn))),
            pl.BlockSpec((1, FF_TILE, D_MODEL), lambda gi, fi, ge, gn, go: (ge[gi], ff(gi, fi, gn), 0)),
            pl.BlockSpec((1, 1, D_MODEL), lambda gi, fi, ge, gn, go: (ge[gi], 0, 0)),
        ],
        out_specs=pl.BlockSpec(memory_space=pl.ANY),
        scratch_shapes=[
            pltpu.VMEM((ROW_GROUP * STAGE_PITCH, LANES), F32),
            pltpu.VMEM((ROW_GROUP, D_MODEL), BF16),
            pltpu.VMEM((ROW_GROUP, D_MODEL), F32),
            pltpu.VMEM((ROW_GROUP // SUB_ROWS, SUB_ROWS * STAGE_PITCH, LANES), F32),
            pltpu.SMEM((IDX_SLOTS * 2 * IDX_LEN,), jnp.int32),
            pltpu.SemaphoreType.DMA((ROW_GROUP // GATHER_ROWS,)), pltpu.SemaphoreType.DMA(()),
            pltpu.SemaphoreType.DMA((IDX_SLOTS,)),
        ],
    )
    return pl.pallas_call(
        _experts_kernel,
        grid_spec=grid_spec,
        out_shape=jax.ShapeDtypeStruct((n_yrows * ROW_C, LANES), F32),
        compiler_params=_cparams(("arbitrary", "arbitrary")),
        name="experts",
    )(g_e, g_rows, g_off, dense2, hn_rows, w_gate, bg3, w_up, bu3, w_down, bd3)


CB_TM = 256


def _combine_kernel(h_ref, gt_ref, g_ref, y0_ref, y1_ref, y2_ref, y3_ref, o_ref, h2_ref):
    gt = gt_ref[...]
    ys = (y0_ref, y1_ref, y2_ref, y3_ref)
    for c in range(ROW_C):
        y = ys[0][pl.ds(c, CB_TM, stride=ROW_C), :] * gt[:, 0:1]
        for kk in range(1, TOP_K):
            y = y + ys[kk][pl.ds(c, CB_TM, stride=ROW_C), :] * gt[:, kk:kk + 1]
        h2_ref[:, c * LANES:(c + 1) * LANES] = h_ref[:, c * LANES:(c + 1) * LANES] + y
    h = h2_ref[...]
    o_ref[...] = h * lax.rsqrt(jnp.mean(h * h, axis=-1, keepdims=True) + NORM_EPS) * g_ref[...]


def _combine(h, gates, g_final, ybuf):
    T = h.shape[0]
    steps = T // CB_TM

    def yspec(kk):
        return pl.BlockSpec((CB_TM * ROW_C, LANES), lambda i: (kk * steps + i, 0))

    return pl.pallas_call(
        _combine_kernel,
        grid=(steps,),
        in_specs=[
            pl.BlockSpec((CB_TM, D_MODEL), lambda i: (i, 0)),
            pl.BlockSpec((CB_TM, TOP_K), lambda i: (i, 0)),
            pl.BlockSpec((1, D_MODEL), lambda i: (0, 0)),
            yspec(0), yspec(1), yspec(2), yspec(3),
        ],
        out_specs=pl.BlockSpec((CB_TM, D_MODEL), lambda i: (i, 0)),
        out_shape=jax.ShapeDtypeStruct((T, D_MODEL), F32),
        scratch_shapes=[pltpu.VMEM((CB_TM, D_MODEL), F32)],
        compiler_params=_cparams(("parallel",)),
        name="combine",
    )(h, gates, g_final, ybuf, ybuf, ybuf, ybuf)


def _group_tables(counts, top_i, n_groups, T):
    A = T * TOP_K
    counts = counts.reshape(N_EXPERTS)
    groups_e = (counts + ROW_GROUP - 1) // ROW_GROUP
    gend = jnp.cumsum(groups_e)
    gstart = gend - groups_e
    total = gend[-1]
    gidx = jnp.arange(n_groups, dtype=jnp.int32)
    used = gidx < total
    last = jnp.maximum(total - 1, 0)
    gsafe = jnp.where(used, gidx, last)
    g_e = jnp.minimum(jnp.sum(gend[None, :] <= gsafe[:, None], axis=1), N_EXPERTS - 1).astype(jnp.int32)
    rows = jnp.clip(counts[g_e] - (gsafe - gstart[g_e]) * ROW_GROUP, 0, ROW_GROUP)
    g_rows = jnp.where(used, rows, 0).astype(jnp.int32)
    cstart = jnp.cumsum(counts) - counts
    g_off = (cstart[g_e] + (gsafe - gstart[g_e]) * ROW_GROUP).astype(jnp.int32)
    a = jnp.arange(A, dtype=jnp.int32)
    dense = lax.sort(top_i.reshape(A) * A + a) % A
    tok = dense // TOP_K
    lists = jnp.stack([tok * ROW_C, ((dense % TOP_K) * T + tok) * ROW_C])
    dense2 = jnp.concatenate([lists, jnp.zeros((2, IDX_LEN), jnp.int32)], axis=1)
    return g_e, g_rows, g_off, dense2


def kernel(x, positions, norm_mix_g, w_in, w_out, rel_bias, norm_ffn_g, router_w, router_b,
           w_gate, b_gate, w_up, b_up, w_down, b_down, norm_final_g):
    B, S, D = x.shape
    T = B * S
    depth = norm_mix_g.shape[0]
    assert depth == 1, "the combine kernel applies the final norm, so exactly one layer is supported"
    n_groups = (T * TOP_K) // ROW_GROUP + N_EXPERTS
    h = x.reshape(T, D)
    pos2 = positions.reshape(T, 1)
    for l in range(depth):
        proj = _in_proj(h, norm_mix_g[l].reshape(1, D), w_in[l].astype(BF16))
        table = _bias_table(rel_bias[l])
        ret = _retention(proj, pos2, B, S)
        att = _attention(proj, table, B, S)
        h, hn_rows, top_i, gates, counts = _out_router(
            h, ret, att, w_out[l].astype(BF16), norm_ffn_g[l].reshape(1, D),
            router_w[l], router_b[l].reshape(1, N_EXPERTS))
        g_e, g_rows, g_off, dense2 = _group_tables(counts, top_i, n_groups, T)
        ybuf = _experts(g_e, g_rows, g_off, dense2, hn_rows, w_gate[l], b_gate[l], w_up[l], b_up[l],
                        w_down[l], b_down[l], n_groups, T * TOP_K)
        h = _combine(h, gates, norm_final_g.reshape(1, D), ybuf)
    return h.reshape(B, S, D)
```
